```python
import jax, jax.numpy as jnp
from jax import lax
import numpy as np

D_MODEL = 1024
BATCH = 8
SEQ = 2048
DEPTH = 2
DEC_BATCH = 8
DEC_SEQ = 32
PAST_LEN = 2048

CHUNK = 64
Q_BLOCK = 128
MIX_WIDTH = D_MODEL
HG_HEADS = 4
HG_DIM = 128
HG_WIDTH = HG_HEADS * HG_DIM
MLA_HEADS = 4
NOPE_DIM = 128
ROPE_DIM = 64
V_DIM = 128
MLA_WIDTH = MLA_HEADS * V_DIM
Q_LORA = 384
KV_LORA = 256
ROPE_BASE = 10000.0
MLA_SCALE = (NOPE_DIM + ROPE_DIM) ** -0.5
D_FF = 2816
N_EXPERTS = 8
TOP_K = 2
D_EXPERT = 3584
N_DENSE = (DEPTH + 1) // 2
N_MOE = DEPTH // 2
EPS = 1e-6
IN_COLS = 4 * HG_WIDTH + Q_LORA + KV_LORA + ROPE_DIM
IN_SPLITS = (HG_WIDTH, 2 * HG_WIDTH, 3 * HG_WIDTH, 4 * HG_WIDTH,
             4 * HG_WIDTH + Q_LORA, 4 * HG_WIDTH + Q_LORA + KV_LORA)

kernel_name = "hgrn2_mla_hybrid_stream_step"


def _rms(x, w):
    xf = x.astype(jnp.float32)
    y = xf * lax.rsqrt(jnp.mean(xf * xf, axis=-1, keepdims=True) + EPS)
    return (y * w.astype(jnp.float32)).astype(x.dtype)


def _rope(x, pos):
    half = ROPE_DIM // 2
    inv_freq = ROPE_BASE ** (-jnp.arange(half, dtype=jnp.float32) / half)
    ang = pos.astype(jnp.float32)[:, None] * inv_freq[None, :]
    cos = jnp.cos(ang)[None, :, None, :]
    sin = jnp.sin(ang)[None, :, None, :]
    xf = x.astype(jnp.float32)
    x1, x2 = xf[..., :half], xf[..., half:]
    return jnp.concatenate([x1 * cos - x2 * sin, x2 * cos + x1 * sin], axis=-1).astype(x.dtype)


def _hgrn_chunk(q, k, v, logf, s0):
    L = q.shape[1]
    b = jnp.cumsum(logf, axis=1)
    o_inter = jnp.einsum('blhk,bhkv->blhv', q * jnp.exp(b), s0)
    causal = jnp.tril(jnp.ones((L, L), dtype=bool))[None, :, :, None, None]
    diff = b[:, :, None] - b[:, None, :]
    decay = jnp.where(causal, jnp.exp(jnp.minimum(diff, 0.0)), 0.0)
    scores = jnp.einsum('btshk,bshk->bhts', q[:, :, None] * decay, k)
    o_intra = jnp.einsum('bhts,bshv->bthv', scores, v)
    b_last = b[:, -1]
    k_dec = k * jnp.exp(b_last[:, None] - b)
    s_new = jnp.exp(b_last)[..., None] * s0 + jnp.einsum('blhk,blhv->bhkv', k_dec, v)
    return o_inter + o_intra, s_new


def _hgrn_scan(q, k, v, logf, s0):
    B, S = q.shape[:2]
    n = S // CHUNK

    def blocks(a):
        return jnp.swapaxes(a.reshape(B, n, CHUNK, *a.shape[2:]), 0, 1)

    def step(s, xs):
        o, s = _hgrn_chunk(xs[0], xs[1], xs[2], xs[3], s)
        return s, o

    s_final, o = lax.scan(step, s0, (blocks(q), blocks(k), blocks(v), blocks(logf)))
    return jnp.swapaxes(o, 0, 1).reshape(B, S, *o.shape[3:]), s_final


def _mla_attend(q_nope, q_pe, k_nope, k_pe, v, mask):
    s = (jnp.einsum('bqhd,bkhd->bhqk', q_nope, k_nope).astype(jnp.float32)
         + jnp.einsum('bqhr,bkr->bhqk', q_pe, k_pe).astype(jnp.float32)) * MLA_SCALE
    if mask is not None:
        s = jnp.where(mask[None, None], s, -1e30)
    p = jax.nn.softmax(s, axis=-1)
    return jnp.einsum('bhqk,bkhv->bqhv', p.astype(v.dtype), v)


def _mla_prompt(q_nope, q_pe, k_nope, k_pe, v):
    B, S = q_nope.shape[:2]
    kchunk = jnp.arange(S) // CHUNK

    def block(i):
        start = i * Q_BLOCK
        qn = lax.dynamic_slice_in_dim(q_nope, start, Q_BLOCK, axis=1)
        qp = lax.dynamic_slice_in_dim(q_pe, start, Q_BLOCK, axis=1)
        qchunk = (start + jnp.arange(Q_BLOCK)) // CHUNK
        mask = kchunk[None, :] <= qchunk[:, None]
        return _mla_attend(qn, qp, k_nope, k_pe, v, mask)

    o = lax.map(block, jnp.arange(S // Q_BLOCK))
    return jnp.swapaxes(o, 0, 1).reshape(B, S, MLA_HEADS, V_DIM)


def _token_mixers(h, pos, lb, hg_state, ckv_past, kpe_past, p, l):
    B, L, _ = h.shape
    f32 = jnp.float32
    proj = jnp.einsum('bld,dn->bln', h, p['w_in'][l])
    hq, hf, hi, hg, cq, ckv, kpe = jnp.split(proj, IN_SPLITS, axis=-1)
    heads = (B, L, HG_HEADS, HG_DIM)
    fl = hf.astype(f32)
    one_minus_f = (1.0 - lb) * jax.nn.sigmoid(-fl)
    logf = jnp.log1p(-one_minus_f).reshape(heads)
    k = one_minus_f.reshape(heads)
    q = jax.nn.silu(hq.astype(f32)).reshape(heads)
    v = hi.astype(f32).reshape(heads)
    if ckv_past is None:
        s0 = jnp.zeros((B, HG_HEADS, HG_DIM, HG_DIM), f32)
        o, s_new = _hgrn_scan(q, k, v, logf, s0)
    else:
        o, s_new = _hgrn_chunk(q, k, v, logf, hg_state.astype(f32))
    o = _rms(o, p['hg_onorm_w'][l]) * jax.nn.silu(hg.astype(f32)).reshape(heads)
    o_hg = o.reshape(B, L, HG_WIDTH).astype(h.dtype)
    cq = _rms(cq, p['mla_qa_norm_w'][l])
    qh = jnp.einsum('blr,rn->bln', cq, p['mla_wqb'][l]).reshape(B, L, MLA_HEADS, NOPE_DIM + ROPE_DIM)
    q_nope = _rms(qh[..., :NOPE_DIM], p['mla_qn_nope_w'][l])
    q_pe = _rope(_rms(qh[..., NOPE_DIM:], p['mla_qn_rope_w'][l]), pos)
    ckv = _rms(ckv, p['mla_kva_norm_w'][l])
    kpe = _rope(_rms(kpe[:, :, None, :], p['mla_kn_rope_w'][l]), pos)[:, :, 0, :]
    if ckv_past is None:
        ckv_all, kpe_all = ckv, kpe
    else:
        ckv_all = jnp.concatenate([ckv_past.astype(ckv.dtype), ckv], axis=1)
        kpe_all = jnp.concatenate([kpe_past.astype(kpe.dtype), kpe], axis=1)
    kv = jnp.einsum('bsr,rn->bsn', ckv_all, p['mla_wkvb'][l]).reshape(B, -1, MLA_HEADS, NOPE_DIM + V_DIM)
    k_nope = _rms(kv[..., :NOPE_DIM], p['mla_kn_nope_w'][l])
    v_mla = kv[..., NOPE_DIM:]
    if ckv_past is None:
        o_mla = _mla_prompt(q_nope, q_pe, k_nope, kpe_all, v_mla)
    else:
        o_mla = _mla_attend(q_nope, q_pe, k_nope, kpe_all, v_mla, None)
    o_mla = _rms(o_mla.reshape(B, L, MLA_WIDTH), p['mla_onorm_w'][l])
    mix = jnp.concatenate([o_hg, o_mla], axis=-1)
    y = jnp.einsum('bln,nd->bld', mix, p['w_o'][l])
    return y, s_new, ckv, kpe


def _swiglu(h, wg, wu, wd):
    a = jax.nn.silu(jnp.einsum('bld,df->blf', h, wg)) * jnp.einsum('bld,df->blf', h, wu)
    return jnp.einsum('blf,fd->bld', a, wd)


def _moe(h, router, wg, wu, wd):
    logits = jnp.einsum('bld,de->ble', h, router).astype(jnp.float32)
    top_v, top_i = lax.top_k(logits, TOP_K)
    gates = jax.nn.softmax(top_v, axis=-1)
    dense_gate = jnp.sum(jax.nn.one_hot(top_i, N_EXPERTS, dtype=jnp.float32) * gates[..., None], axis=-2)
    out = jnp.zeros_like(h)
    for e in range(N_EXPERTS):
        out = out + dense_gate[..., e:e + 1].astype(h.dtype) * _swiglu(h, wg[e], wu[e], wd[e])
    return out


def _trunk(x, c, pos, hg_state, ckv_past, kpe_past, p, lower):
    sc = jax.nn.silu(c)
    new_s, new_ckv, new_kpe = [], [], []
    for l in range(DEPTH):
        mod = jnp.einsum('bd,dn->bn', sc, p['ada_w'][l]) + p['ada_b'][l]
        sh1, sc1, g1, sh2, sc2, g2 = [m[:, None, :] for m in jnp.split(mod, 6, axis=-1)]
        h = _rms(x, p['norm1_w'][l]) * (1 + sc1) + sh1
        y, s_l, ckv_l, kpe_l = _token_mixers(
            h, pos, lower[l],
            None if hg_state is None else hg_state[l],
            None if ckv_past is None else ckv_past[l],
            None if kpe_past is None else kpe_past[l], p, l)
        x = x + g1 * y
        h = _rms(x, p['norm2_w'][l]) * (1 + sc2) + sh2
        j = l // 2
        if l % 2 == 0:
            f = _swiglu(h, p['ffn_w_gate'][j], p['ffn_w_up'][j], p['ffn_w_down'][j])
        else:
            f = _moe(h, p['moe_router'][j], p['moe_w_gate'][j], p['moe_w_up'][j], p['moe_w_down'][j])
        x = x + g2 * f
        new_s.append(s_l.astype(x.dtype))
        new_ckv.append(ckv_l)
        new_kpe.append(kpe_l)
    return x, jnp.stack(new_s), jnp.stack(new_ckv), jnp.stack(new_kpe)


def setup_inputs(seed: int = 0) -> dict:
    key = jax.random.key(seed)
    ks = iter(list(jax.random.split(key, 48)))
    D = D_MODEL

    def nrm(shape, scale):
        return scale * jax.random.normal(next(ks), shape, jnp.float32)

    def gain(shape):
        return 1.0 + 0.05 * jax.random.normal(next(ks), shape, jnp.float32)

    return {
        'x_prompt': nrm((BATCH, SEQ, D), 1.0),
        'x_sample': nrm((DEC_BATCH, DEC_SEQ, D), 1.0),
        'c_prompt': nrm((BATCH, D), 1.0),
        'c_sample': nrm((DEC_BATCH, D), 1.0),
        'state_hgrn': nrm((DEPTH, DEC_BATCH, HG_HEADS, HG_DIM, HG_DIM), 0.5),
        'cache_ckv': nrm((DEPTH, DEC_BATCH, PAST_LEN, KV_LORA), 1.0),
        'cache_kpe': nrm((DEPTH, DEC_BATCH, PAST_LEN, ROPE_DIM), 1.0),
        'ada_w': nrm((DEPTH, D, 6 * D), 0.5 * D ** -0.5),
        'ada_b': nrm((DEPTH, 6 * D), 0.02),
        'norm1_w': gain((DEPTH, D)),
        'norm2_w': gain((DEPTH, D)),
        'w_in': nrm((DEPTH, D, IN_COLS), D ** -0.5),
        'hg_lower_bounds': nrm((DEPTH, HG_WIDTH), 0.5),
        'hg_onorm_w': gain((DEPTH, HG_DIM)),
        'mla_qa_norm_w': gain((DEPTH, Q_LORA)),
        'mla_wqb': nrm((DEPTH, Q_LORA, MLA_HEADS * (NOPE_DIM + ROPE_DIM)), Q_LORA ** -0.5),
        'mla_kva_norm_w': gain((DEPTH, KV_LORA)),
        'mla_wkvb': nrm((DEPTH, KV_LORA, MLA_HEADS * (NOPE_DIM + V_DIM)), KV_LORA ** -0.5),
        'mla_qn_nope_w': gain((DEPTH, NOPE_DIM)),
        'mla_qn_rope_w': gain((DEPTH, ROPE_DIM)),
        'mla_kn_nope_w': gain((DEPTH, NOPE_DIM)),
        'mla_kn_rope_w': gain((DEPTH, ROPE_DIM)),
        'mla_onorm_w': gain((DEPTH, MLA_WIDTH)),
        'w_o': nrm((DEPTH, MIX_WIDTH, D), MIX_WIDTH ** -0.5),
        'ffn_w_gate': nrm((N_DENSE, D, D_FF), D ** -0.5),
        'ffn_w_up': nrm((N_DENSE, D, D_FF), D ** -0.5),
        'ffn_w_down': nrm((N_DENSE, D_FF, D), D_FF ** -0.5),
        'moe_router': nrm((N_MOE, D, N_EXPERTS), D ** -0.5),
        'moe_w_gate': nrm((N_MOE, N_EXPERTS, D, D_EXPERT), D ** -0.5),
        'moe_w_up': nrm((N_MOE, N_EXPERTS, D, D_EXPERT), D ** -0.5),
        'moe_w_down': nrm((N_MOE, N_EXPERTS, D_EXPERT, D), D_EXPERT ** -0.5),
    }


def reference(x_prompt, x_sample, c_prompt, c_sample, state_hgrn, cache_ckv, cache_kpe,
              ada_w, ada_b, norm1_w, norm2_w, w_in, hg_lower_bounds, hg_onorm_w,
              mla_qa_norm_w, mla_wqb, mla_kva_norm_w, mla_wkvb, mla_qn_nope_w, mla_qn_rope_w,
              mla_kn_nope_w, mla_kn_rope_w, mla_onorm_w, w_o,
              ffn_w_gate, ffn_w_up, ffn_w_down,
              moe_router, moe_w_gate, moe_w_up, moe_w_down):
    p = dict(ada_w=ada_w, ada_b=ada_b, norm1_w=norm1_w, norm2_w=norm2_w, w_in=w_in,
             hg_onorm_w=hg_onorm_w, mla_qa_norm_w=mla_qa_norm_w, mla_wqb=mla_wqb,
             mla_kva_norm_w=mla_kva_norm_w, mla_wkvb=mla_wkvb, mla_qn_nope_w=mla_qn_nope_w,
             mla_qn_rope_w=mla_qn_rope_w, mla_kn_nope_w=mla_kn_nope_w, mla_kn_rope_w=mla_kn_rope_w,
             mla_onorm_w=mla_onorm_w, w_o=w_o, ffn_w_gate=ffn_w_gate, ffn_w_up=ffn_w_up,
             ffn_w_down=ffn_w_down, moe_router=moe_router, moe_w_gate=moe_w_gate,
             moe_w_up=moe_w_up, moe_w_down=moe_w_down)
    lw = jax.nn.softmax(hg_lower_bounds.astype(jnp.float32), axis=0)
    lower = jnp.clip(jnp.cumsum(lw, axis=0) - lw[0], 0.0, 1.0)
    pos_p = jnp.arange(x_prompt.shape[1])
    pos_s = cache_ckv.shape[2] + jnp.arange(x_sample.shape[1])
    y_prompt, s_p, ckv_p, kpe_p = _trunk(x_prompt, c_prompt, pos_p, None, None, None, p, lower)
    y_sample, s_s, ckv_s, kpe_s = _trunk(x_sample, c_sample, pos_s, state_hgrn, cache_ckv, cache_kpe, p, lower)
    return (y_prompt, y_sample, s_p, ckv_p, kpe_p, s_s, ckv_s, kpe_s)
```

```python
import functools
import math

import numpy as np
import jax
import jax.numpy as jnp
from jax import lax
from jax.experimental import pallas as pl
from jax.experimental.pallas import tpu as pltpu

F32 = jnp.float32
BF16 = jnp.bfloat16
EPS = 1e-6
ROPE_BASE = 10000.0
GROUP = 32
LANES = 128
HG_HEADS = 4
HG_DIM = 128
MLA_HEADS = 4
NOPE = 128
ROPE = 64
VDIM = 128
QK_PAD = 256
CHUNK = 64
N_EXPERTS = 8
VMEM_LIMIT = 56 * 1024 * 1024

_NT = (((1,), (1,)), ((), ()))
_TN = (((0,), (0,)), ((), ()))


def _dot(a, b):
    return jnp.dot(a, b, preferred_element_type=F32)


def _dot_nt(a, b):
    return lax.dot_general(a, b, _NT, preferred_element_type=F32)


def _dot_tn(a, b):
    return lax.dot_general(a, b, _TN, preferred_element_type=F32)


def _rms(x, w):
    return x * lax.rsqrt(jnp.mean(x * x, axis=-1, keepdims=True) + EPS) * w


def _silu(x):
    return x * jax.nn.sigmoid(x)


def _split_bf16(x):
    hi = x.astype(BF16)
    lo = (x - hi.astype(F32)).astype(BF16)
    return hi, lo


def _pick_tile(n, target, mult):
    best = None
    for t in range(mult, min(n, target) + 1, mult):
        if n % t == 0:
            best = t
    assert best is not None, (n, target, mult)
    return best


def _params(*sem):
    return pltpu.CompilerParams(dimension_semantics=sem, vmem_limit_bytes=VMEM_LIMIT)


def _mod_kernel(c_ref, w_ref, b_ref, o_ref):
    sc = _silu(c_ref[...]).astype(BF16)
    o_ref[0] = _dot(sc, w_ref[0].astype(BF16)) + b_ref[0]


def _modulation(c_all, ada_w, ada_b):
    depth, d, n6 = ada_w.shape
    rows = c_all.shape[0]
    tn = _pick_tile(n6, 1536, LANES)
    return pl.pallas_call(
        _mod_kernel,
        out_shape=jax.ShapeDtypeStruct((depth, rows, n6), F32),
        grid=(depth, n6 // tn),
        in_specs=[
            pl.BlockSpec((rows, d), lambda l, j: (0, 0)),
            pl.BlockSpec((1, d, tn), lambda l, j: (l, 0, j)),
            pl.BlockSpec((1, 1, tn), lambda l, j: (l, 0, j)),
        ],
        out_specs=pl.BlockSpec((1, rows, tn), lambda l, j: (l, 0, j)),
        compiler_params=_params("arbitrary", "arbitrary"),
    )(c_all, ada_w, ada_b.reshape(depth, 1, n6))


def _rope128(g, w128, table):
    t = _rms(g, w128) * table
    return t + pltpu.roll(t, ROPE, 1)


def _front_kernel(x_ref, mod_ref, n1w_ref, win_ref, lb_ref, qaw_ref, wqb_ref, qnw_ref, qrw_ref,
                  kvaw_ref, wkvb_ref, knw_ref, krw_ref, rope_ref,
                  hq_ref, hk_ref, lf_ref, hv_ref, hg_ref, ckv_ref, kpe_ref, q_ref, k_ref, v_ref,
                  h_scr, *, groups, q_lora, kv_lora, scale):
    d = x_ref.shape[1]
    hw = HG_HEADS * HG_DIM
    n1w = n1w_ref[...]
    for g in range(groups):
        rows = slice(g * GROUP, (g + 1) * GROUP)
        shift = mod_ref[g:g + 1, 0:d]
        gain = mod_ref[g:g + 1, d:2 * d]
        h_scr[rows, :] = (_rms(x_ref[rows, :], n1w) * (1.0 + gain) + shift).astype(BF16)
    h = h_scr[...]

    hq_ref[...] = _silu(_dot(h, win_ref[:, 0:hw]))
    one_minus_f = (1.0 - lb_ref[...]) * jax.nn.sigmoid(-_dot(h, win_ref[:, hw:2 * hw]))
    hk_ref[...] = one_minus_f
    lf_ref[...] = jnp.log1p(-one_minus_f)
    hv_ref[...] = _dot(h, win_ref[:, 2 * hw:3 * hw])
    hg_ref[...] = _silu(_dot(h, win_ref[:, 3 * hw:4 * hw]))

    o = 4 * hw
    table = rope_ref[...]
    cq = _rms(_dot(h, win_ref[:, o:o + q_lora]), qaw_ref[...]).astype(BF16)
    qh = _dot(cq, wqb_ref[...])
    nw = MLA_HEADS * NOPE
    for hd in range(MLA_HEADS):
        qn = _rms(qh[:, hd * NOPE:(hd + 1) * NOPE], qnw_ref[...]) * scale
        qr = _rope128(qh[:, nw + hd * LANES:nw + (hd + 1) * LANES], qrw_ref[...], table) * scale
        q_ref[:, hd * QK_PAD:hd * QK_PAD + NOPE] = qn.astype(BF16)
        q_ref[:, hd * QK_PAD + NOPE:(hd + 1) * QK_PAD] = qr.astype(BF16)

    o += q_lora
    ckv = _rms(_dot(h, win_ref[:, o:o + kv_lora]), kvaw_ref[...])
    ckv_ref[...] = ckv
    o += kv_lora
    kp = _rope128(_dot(h, win_ref[:, o:o + LANES]), krw_ref[...], table)
    kpe_ref[...] = kp[:, 0:ROPE]
    lane = lax.broadcasted_iota(jnp.int32, kp.shape, 1)
    kp_pad = jnp.where(lane < ROPE, kp, 0.0).astype(BF16)
    kv = _dot(ckv.astype(BF16), wkvb_ref[...])
    for hd in range(MLA_HEADS):
        c0 = hd * (NOPE + VDIM)
        k_ref[:, hd * QK_PAD:hd * QK_PAD + NOPE] = _rms(kv[:, c0:c0 + NOPE], knw_ref[...]).astype(BF16)
        k_ref[:, hd * QK_PAD + NOPE:(hd + 1) * QK_PAD] = kp_pad
        v_ref[:, hd * VDIM:(hd + 1) * VDIM] = kv[:, c0 + NOPE:c0 + NOPE + VDIM].astype(BF16)


def _front(x, mod_g, n1w, w_in, lb, qaw, wqb, qnw, qrw, kvaw, wkvb, knw, krw, rope, scale):
    n, d = x.shape
    tm = _pick_tile(n, 256, GROUP)
    groups = tm // GROUP
    hw = HG_HEADS * HG_DIM
    q_lora, kv_lora = qaw.shape[1], kvaw.shape[1]
    row = lambda i: (i, 0)
    fixed = lambda i: (0, 0)

    def full(a):
        return pl.BlockSpec(a.shape, fixed)

    f32_out = lambda w: jax.ShapeDtypeStruct((n, w), F32)
    out_shape = [f32_out(hw)] * 5 + [f32_out(kv_lora), f32_out(ROPE),
                                     jax.ShapeDtypeStruct((n, MLA_HEADS * QK_PAD), BF16),
                                     jax.ShapeDtypeStruct((n, MLA_HEADS * QK_PAD), BF16),
                                     jax.ShapeDtypeStruct((n, MLA_HEADS * VDIM), BF16)]
    out_specs = [pl.BlockSpec((tm, s.shape[1]), row) for s in out_shape]
    kern = functools.partial(_front_kernel, groups=groups, q_lora=q_lora, kv_lora=kv_lora, scale=scale)
    return pl.pallas_call(
        kern,
        out_shape=out_shape,
        grid=(n // tm,),
        in_specs=[
            pl.BlockSpec((tm, d), row),
            pl.BlockSpec((groups, 2 * d), row),
            full(n1w), full(w_in), full(lb), full(qaw), full(wqb), full(qnw), full(qrw),
            full(kvaw), full(wkvb), full(knw), full(krw),
            pl.BlockSpec((tm, LANES), row),
        ],
        out_specs=out_specs,
        scratch_shapes=[pltpu.VMEM((tm, d), BF16)],
        compiler_params=_params("arbitrary"),
    )(x, mod_g, n1w, w_in, lb, qaw, wqb, qnw, qrw, kvaw, wkvb, knw, krw, rope)


def _kvup_kernel(ckv_ref, kpe_ref, wkvb_ref, knw_ref, k_ref, v_ref):
    kv = _dot(ckv_ref[...].astype(BF16), wkvb_ref[...])
    kp = kpe_ref[...].astype(BF16)
    for hd in range(MLA_HEADS):
        c0 = hd * (NOPE + VDIM)
        k_ref[:, hd * QK_PAD:hd * QK_PAD + NOPE] = _rms(kv[:, c0:c0 + NOPE], knw_ref[...]).astype(BF16)
        k_ref[:, hd * QK_PAD + NOPE:(hd + 1) * QK_PAD] = kp
        v_ref[:, hd * VDIM:(hd + 1) * VDIM] = kv[:, c0 + NOPE:c0 + NOPE + VDIM].astype(BF16)


def _kv_up(ckv, kpe_pad, wkvb, knw):
    n, r = ckv.shape
    tm = _pick_tile(n, 512, 16)
    row = lambda i: (i, 0)
    fixed = lambda i: (0, 0)
    return pl.pallas_call(
        _kvup_kernel,
        out_shape=[jax.ShapeDtypeStruct((n, MLA_HEADS * QK_PAD), BF16),
                   jax.ShapeDtypeStruct((n, MLA_HEADS * VDIM), BF16)],
        grid=(n // tm,),
        in_specs=[pl.BlockSpec((tm, r), row), pl.BlockSpec((tm, LANES), row),
                  pl.BlockSpec(wkvb.shape, fixed), pl.BlockSpec(knw.shape, fixed)],
        out_specs=[pl.BlockSpec((tm, MLA_HEADS * QK_PAD), row), pl.BlockSpec((tm, MLA_HEADS * VDIM), row)],
        compiler_params=_params("arbitrary"),
    )(ckv, kpe_pad, wkvb, knw)


def _hgrn_consts(length):
    levels = int(math.log2(length))
    assert 1 << levels == length
    expo = np.zeros(((levels + 2) * length, length), np.float32)
    idx = np.arange(length)
    for t in range(length):
        expo[t, :t + 1] = 1.0
        expo[length + t, t + 1:] = 1.0
    masks = np.zeros((levels + 1, length, length), np.float32)
    xor = idx[:, None] ^ idx[None, :]
    for m in range(levels):
        block = length >> m
        half = block >> 1
        for t in range(length):
            mid = (t // block) * block + half - 1
            if t > mid:
                expo[(2 + m) * length + t, mid + 1:t + 1] = 1.0
            else:
                expo[(2 + m) * length + t, t + 1:mid + 1] = 1.0
        masks[m] = (idx[:, None] > idx[None, :]) & (xor >= half) & (xor < block)
    masks[levels] = np.eye(length)
    return jnp.asarray(expo, BF16), jnp.asarray(masks, F32)


def _hgrn_kernel(q_ref, k_ref, lf_ref, v_ref, g_ref, s0_ref, expo_ref, mask_ref, onw_ref, alias_ref,
                 o_ref, s_ref, st_scr, *, length, levels):
    del alias_ref
    c = pl.program_id(1)

    @pl.when(c == 0)
    def _():
        for hd in range(HG_HEADS):
            st_scr[hd] = s0_ref[0, hd].T

    row_idx = lax.broadcasted_iota(jnp.int32, (length, HG_DIM), 0)
    expo = expo_ref[...]
    for hd in range(HG_HEADS):
        cols = slice(hd * HG_DIM, (hd + 1) * HG_DIM)
        lf_hi, lf_lo = _split_bf16(lf_ref[:, cols])
        decay = jnp.exp(_dot(expo, lf_hi) + _dot(expo, lf_lo))
        q = q_ref[:, cols]
        k = k_ref[:, cols]
        vb = v_ref[:, cols].astype(BF16)
        st = st_scr[hd]
        o = _dot_nt((q * decay[0:length]).astype(BF16), st.astype(BF16))
        scores = _dot_nt(q.astype(BF16), k.astype(BF16)) * mask_ref[levels]
        for m in range(levels):
            half = length >> (m + 1)
            side = jnp.where((row_idx & half) != 0, q, k)
            xm = (side * decay[(2 + m) * length:(3 + m) * length]).astype(BF16)
            scores = scores + _dot_nt(xm, xm) * mask_ref[m]
        o = o + _dot(scores.astype(BF16), vb)
        k_dec = (k * decay[length:2 * length]).astype(BF16)
        st_scr[hd] = st * decay[length - 1:length] + _dot_tn(vb, k_dec)
        o_ref[:, cols] = (_rms(o, onw_ref[...]) * g_ref[:, cols]).astype(BF16)

    @pl.when(c == pl.num_programs(1) - 1)
    def _():
        for hd in range(HG_HEADS):
            s_ref[0, hd] = st_scr[hd].T


def _hgrn(hq, hk, lf, hv, hg, s0, onw, o_alias, row0, seq):
    n, hw = hq.shape
    b = s0.shape[0]
    length = 1
    while length * 2 <= min(seq, 128) and seq % (length * 2) == 0:
        length *= 2
    levels = int(math.log2(length))
    nc = seq // length
    assert row0 % length == 0
    blk0 = row0 // length
    expo, masks = _hgrn_consts(length)
    row = lambda i, c: (blk0 + i * nc + c, 0)
    rows = pl.BlockSpec((length, hw), row)
    state = pl.BlockSpec((1, HG_HEADS, HG_DIM, HG_DIM), lambda i, c: (i, 0, 0, 0))
    kern = functools.partial(_hgrn_kernel, length=length, levels=levels)
    return pl.pallas_call(
        kern,
        out_shape=[jax.ShapeDtypeStruct((n, hw), BF16), jax.ShapeDtypeStruct(s0.shape, F32)],
        grid=(b, nc),
        in_specs=[rows, rows, rows, rows, rows, state,
                  pl.BlockSpec(expo.shape, lambda i, c: (0, 0)),
                  pl.BlockSpec(masks.shape, lambda i, c: (0, 0, 0)),
                  pl.BlockSpec(onw.shape, lambda i, c: (0, 0)),
                  pl.BlockSpec(memory_space=pl.ANY)],
        out_specs=[rows, state],
        scratch_shapes=[pltpu.VMEM((HG_HEADS, HG_DIM, HG_DIM), F32)],
        input_output_aliases={9: 0},
        compiler_params=_params("arbitrary", "arbitrary"),
    )(hq, hk, lf, hv, hg, s0, expo, masks, onw, o_alias)


def _attn_prompt_kernel(q_ref, k_ref, v_ref, alias_ref, o_ref, *, tq):
    del alias_ref
    i = pl.program_id(2)
    q = q_ref[...]

    def step(s, vj, carry):
        m, l, acc = carry
        m_new = jnp.maximum(m, jnp.max(s, axis=-1, keepdims=True))
        a = jnp.exp(m - m_new)
        p = jnp.exp(s - m_new)
        return m_new, a * l + jnp.sum(p, axis=-1, keepdims=True), a * acc + _dot(p.astype(BF16), vj)

    def body(j, carry):
        r0 = pl.multiple_of(j * tq, tq)
        return step(_dot_nt(q, k_ref[pl.ds(r0, tq), :]), v_ref[pl.ds(r0, tq), :], carry)

    carry = (jnp.full((tq, 1), -1e30, F32), jnp.zeros((tq, 1), F32), jnp.zeros((tq, VDIM), F32))
    carry = lax.fori_loop(0, i, body, carry)
    r0 = pl.multiple_of(i * tq, tq)
    s = _dot_nt(q, k_ref[pl.ds(r0, tq), :])
    qc = lax.broadcasted_iota(jnp.int32, (tq, tq), 0) // CHUNK
    kc = lax.broadcasted_iota(jnp.int32, (tq, tq), 1) // CHUNK
    s = jnp.where(kc <= qc, s, -1e30)
    _, l, acc = step(s, v_ref[pl.ds(r0, tq), :], carry)
    o_ref[...] = acc / l


def _attn_prompt(q_all, k_all, v_all, o_alias, b, seq):
    n = q_all.shape[0]
    tq = _pick_tile(seq, 256, CHUNK)
    nq = seq // tq
    kern = functools.partial(_attn_prompt_kernel, tq=tq)
    return pl.pallas_call(
        kern,
        out_shape=jax.ShapeDtypeStruct((n, MLA_HEADS * VDIM), F32),
        grid=(b, MLA_HEADS, nq),
        in_specs=[pl.BlockSpec((tq, QK_PAD), lambda bi, h, i: (bi * nq + i, h)),
                  pl.BlockSpec((seq, QK_PAD), lambda bi, h, i: (bi, h)),
                  pl.BlockSpec((seq, VDIM), lambda bi, h, i: (bi, h)),
                  pl.BlockSpec(memory_space=pl.ANY)],
        out_specs=pl.BlockSpec((tq, VDIM), lambda bi, h, i: (bi * nq + i, h)),
        input_output_aliases={3: 0},
        compiler_params=_params("arbitrary", "arbitrary", "arbitrary"),
    )(q_all, k_all, v_all, o_alias)


def _attn_sample_kernel(q_ref, kp_ref, vp_ref, kn_ref, vn_ref, alias_ref, o_ref):
    del alias_ref
    q = q_ref[...]
    sp = _dot_nt(q, kp_ref[...])
    sn = _dot_nt(q, kn_ref[...])
    m = jnp.maximum(jnp.max(sp, axis=-1, keepdims=True), jnp.max(sn, axis=-1, keepdims=True))
    pp = jnp.exp(sp - m)
    pn = jnp.exp(sn - m)
    l = jnp.sum(pp, axis=-1, keepdims=True) + jnp.sum(pn, axis=-1, keepdims=True)
    acc = _dot(pp.astype(BF16), vp_ref[...]) + _dot(pn.astype(BF16), vn_ref[...])
    o_ref[...] = acc / l


def _attn_sample(q_all, k_all, v_all, k_past, v_past, o_alias, b, seq, past, row0):
    assert row0 % seq == 0
    blk0 = row0 // seq
    new_q = lambda bi, h: (blk0 + bi, h)
    return pl.pallas_call(
        _attn_sample_kernel,
        out_shape=jax.ShapeDtypeStruct(o_alias.shape, F32),
        grid=(b, MLA_HEADS),
        in_specs=[pl.BlockSpec((seq, QK_PAD), new_q),
                  pl.BlockSpec((past, QK_PAD), lambda bi, h: (bi, h)),
                  pl.BlockSpec((past, VDIM), lambda bi, h: (bi, h)),
                  pl.BlockSpec((seq, QK_PAD), new_q),
                  pl.BlockSpec((seq, VDIM), new_q),
                  pl.BlockSpec(memory_space=pl.ANY)],
        out_specs=pl.BlockSpec((seq, VDIM), new_q),
        input_output_aliases={5: 0},
        compiler_params=_params("arbitrary", "arbitrary"),
    )(q_all, k_past, v_past, k_all, v_all, o_alias)


def _out_kernel(ohg_ref, omla_ref, x_ref, g1_ref, sh2_ref, sc2_ref, monw_ref, wo_ref, n2w_ref,
                x1_ref, h2_ref, *, groups):
    hw = ohg_ref.shape[1]
    om = _rms(omla_ref[...], monw_ref[...]).astype(BF16)
    y = _dot(ohg_ref[...], wo_ref[0:hw, :]) + _dot(om, wo_ref[hw:, :])
    n2w = n2w_ref[...]
    for g in range(groups):
        rows = slice(g * GROUP, (g + 1) * GROUP)
        x1 = x_ref[rows, :] + g1_ref[g:g + 1, :] * y[rows, :]
        x1_ref[rows, :] = x1
        h2 = _rms(x1, n2w) * (1.0 + sc2_ref[g:g + 1, :]) + sh2_ref[g:g + 1, :]
        h2_ref[rows, :] = h2.astype(h2_ref.dtype)


def _out_proj(o_hg, o_mla, x, mod_g, monw, w_o, n2w, h2_dtype):
    n, d = x.shape
    tm = _pick_tile(n, 256, GROUP)
    groups = tm // GROUP
    row = lambda i: (i, 0)
    fixed = lambda i: (0, 0)
    modspec = lambda col: pl.BlockSpec((groups, d), lambda i: (i, col))
    return pl.pallas_call(
        functools.partial(_out_kernel, groups=groups),
        out_shape=[jax.ShapeDtypeStruct((n, d), F32), jax.ShapeDtypeStruct((n, d), h2_dtype)],
        grid=(n // tm,),
        in_specs=[pl.BlockSpec((tm, o_hg.shape[1]), row), pl.BlockSpec((tm, o_mla.shape[1]), row),
                  pl.BlockSpec((tm, d), row), modspec(2), modspec(3), modspec(4),
                  pl.BlockSpec(monw.shape, fixed), pl.BlockSpec(w_o.shape, fixed),
                  pl.BlockSpec(n2w.shape, fixed)],
        out_specs=[pl.BlockSpec((tm, d), row), pl.BlockSpec((tm, d), row)],
        compiler_params=_params("arbitrary"),
    )(o_hg, o_mla, x, mod_g, mod_g, mod_g, monw, w_o, n2w)


def _ffn_kernel(h_ref, wg_ref, wu_ref, wd_ref, x1_ref, g2_ref, o_ref, acc_scr, *, groups):
    j = pl.program_id(1)

    @pl.when(j == 0)
    def _():
        acc_scr[...] = jnp.zeros_like(acc_scr)

    h = h_ref[...]
    a = (_silu(_dot(h, wg_ref[...])) * _dot(h, wu_ref[...])).astype(BF16)
    acc_scr[...] += _dot(a, wd_ref[...])

    @pl.when(j == pl.num_programs(1) - 1)
    def _():
        for g in range(groups):
            rows = slice(g * GROUP, (g + 1) * GROUP)
            o_ref[rows, :] = x1_ref[rows, :] + g2_ref[g:g + 1, :] * acc_scr[rows, :]


def _ffn(h2, wg, wu, wd, x1, mod_g):
    n, d = x1.shape
    f = wg.shape[1]
    tm = _pick_tile(n, 1280, GROUP)
    tf = _pick_tile(f, 512, LANES)
    groups = tm // GROUP
    return pl.pallas_call(
        functools.partial(_ffn_kernel, groups=groups),
        out_shape=jax.ShapeDtypeStruct((n, d), F32),
        grid=(n // tm, f // tf),
        in_specs=[pl.BlockSpec((tm, d), lambda i, j: (i, 0)),
                  pl.BlockSpec((d, tf), lambda i, j: (0, j)),
                  pl.BlockSpec((d, tf), lambda i, j: (0, j)),
                  pl.BlockSpec((tf, d), lambda i, j: (j, 0)),
                  pl.BlockSpec((tm, d), lambda i, j: (i, 0)),
                  pl.BlockSpec((groups, d), lambda i, j: (i, 5))],
        out_specs=pl.BlockSpec((tm, d), lambda i, j: (i, 0)),
        scratch_shapes=[pltpu.VMEM((tm, d), F32)],
        compiler_params=_params("arbitrary", "arbitrary"),
    )(h2, wg, wu, wd, x1, mod_g)


def _router_kernel(h_ref, rhi_ref, rlo_ref, tril_ref, pos_ref, gate_ref, cnt_ref, cnt_scr):
    i = pl.program_id(0)

    @pl.when(i == 0)
    def _():
        cnt_scr[...] = jnp.zeros_like(cnt_scr)

    h_hi, h_lo = _split_bf16(h_ref[...])
    logits = _dot(h_hi, rhi_ref[...]) + (_dot(h_hi, rlo_ref[...]) + _dot(h_lo, rhi_ref[...]))
    lane = lax.broadcasted_iota(jnp.int32, logits.shape, 1).astype(F32)
    logits = jnp.where(lane < N_EXPERTS, logits, -1e30)
    m1 = jnp.max(logits, axis=-1, keepdims=True)
    i1 = jnp.min(jnp.where(logits == m1, lane, float(LANES)), axis=-1, keepdims=True)
    hot1 = lane == i1
    rest = jnp.where(hot1, -1e30, logits)
    m2 = jnp.max(rest, axis=-1, keepdims=True)
    i2 = jnp.min(jnp.where(rest == m2, lane, float(LANES)), axis=-1, keepdims=True)
    hot2 = lane == i2
    e = jnp.exp(m2 - m1)
    gate1 = 1.0 / (1.0 + e)
    gate2 = e / (1.0 + e)
    gate_ref[...] = jnp.where(lane == 0, gate1, jnp.where(lane == 1, gate2, 0.0))
    hot = jnp.where(hot1, 1.0, jnp.where(hot2, 1.0, 0.0))
    before = _dot(tril_ref[...], hot.astype(BF16)) + cnt_scr[0:1, :]
    rank1 = jnp.sum(jnp.where(hot1, before, 0.0), axis=-1, keepdims=True)
    rank2 = jnp.sum(jnp.where(hot2, before, 0.0), axis=-1, keepdims=True)
    info = jnp.where(lane == 0, rank1, jnp.where(lane == 1, rank2,
                     jnp.where(lane == 2, i1, jnp.where(lane == 3, i2, 0.0))))
    pos_ref[...] = info.astype(jnp.int32)
    total = cnt_scr[0:1, :] + jnp.sum(hot, axis=0, keepdims=True)
    cnt_scr[...] = jnp.broadcast_to(total, cnt_scr.shape)
    cnt_ref[...] = jnp.broadcast_to(total, cnt_ref.shape).astype(jnp.int32)


def _router(h2, router):
    n, d = h2.shape
    tm = _pick_tile(n, 256, GROUP)
    r_pad = jnp.zeros((d, LANES), F32).at[:, :N_EXPERTS].set(router)
    r_hi = r_pad.astype(BF16)
    r_lo = (r_pad - r_hi.astype(F32)).astype(BF16)
    tril = jnp.asarray(np.tril(np.ones((tm, tm), np.float32), -1), BF16)
    row = lambda i: (i, 0)
    fixed = lambda i: (0, 0)
    return pl.pallas_call(
        _router_kernel,
        out_shape=[jax.ShapeDtypeStruct((n, LANES), jnp.int32), jax.ShapeDtypeStruct((n, LANES), F32),
                   jax.ShapeDtypeStruct((8, LANES), jnp.int32)],
        grid=(n // tm,),
        in_specs=[pl.BlockSpec((tm, d), row), pl.BlockSpec((d, LANES), fixed),
                  pl.BlockSpec((d, LANES), fixed), pl.BlockSpec((tm, tm), fixed)],
        out_specs=[pl.BlockSpec((tm, LANES), row), pl.BlockSpec((tm, LANES), row),
                   pl.BlockSpec((8, LANES), fixed)],
        scratch_shapes=[pltpu.VMEM((8, LANES), F32)],
        compiler_params=_params("arbitrary"),
    )(h2, r_hi, r_lo, tril)


def _row_copy(src, src_row, dst, dst_row, sem):
    return pltpu.make_async_copy(src.at[pl.ds(src_row, 1), :], dst.at[pl.ds(dst_row, 1), :], sem)


def _scatter_kernel(cnt_ref, off_ref, nu_ref, pos_ref, h_ref, xs_ref, zero_scr, sem, *, tm, tile, n_tiles):
    i = pl.program_id(0)

    def start(r, _):
        _row_copy(h_ref, r, xs_ref, pos_ref[0, 0, 2 * r], sem).start()
        _row_copy(h_ref, r, xs_ref, pos_ref[0, 0, 2 * r + 1], sem).start()
        return 0

    def wait(r, _):
        _row_copy(h_ref, 0, xs_ref, 0, sem).wait()
        return 0

    lax.fori_loop(0, tm, start, 0)
    lax.fori_loop(0, 2 * tm, wait, 0)

    @pl.when(i == pl.num_programs(0) - 1)
    def _():
        zero_scr[...] = jnp.zeros_like(zero_scr)
        for e in range(N_EXPERTS):
            cnt = cnt_ref[e]
            end = ((cnt + tile - 1) // tile) * tile

            def fill(r, _):
                _row_copy(zero_scr, 0, xs_ref, off_ref[e] + r, sem).start()
                return 0

            def fill_wait(r, _):
                _row_copy(zero_scr, 0, xs_ref, 0, sem).wait()
                return 0

            lax.fori_loop(cnt, end, fill, 0)
            lax.fori_loop(cnt, end, fill_wait, 0)

        def fill_tile(t, _):
            cp = pltpu.make_async_copy(zero_scr, xs_ref.at[pl.ds(pl.multiple_of(t * tile, tile), tile), :], sem)
            cp.start()
            cp.wait()
            return 0

        lax.fori_loop(nu_ref[0], n_tiles, fill_tile, 0)


def _scatter(h2, pos2, counts, offsets, n_used, tile, n_tiles):
    n, d = h2.shape
    tm = pos2.shape[2] // 2
    grid_spec = pltpu.PrefetchScalarGridSpec(
        num_scalar_prefetch=3,
        grid=(n // tm,),
        in_specs=[pl.BlockSpec((1, 1, 2 * tm), lambda i, *_: (i, 0, 0), memory_space=pltpu.SMEM),
                  pl.BlockSpec((tm, d), lambda i, *_: (i, 0))],
        out_specs=pl.BlockSpec(memory_space=pl.ANY),
        scratch_shapes=[pltpu.VMEM((tile, d), F32), pltpu.SemaphoreType.DMA(())],
    )
    return pl.pallas_call(
        functools.partial(_scatter_kernel, tm=tm, tile=tile, n_tiles=n_tiles),
        out_shape=jax.ShapeDtypeStruct((n_tiles * tile, d), F32),
        grid_spec=grid_spec,
        compiler_params=_params("arbitrary"),
    )(counts, offsets, n_used, pos2, h2)


def _moe_kernel(te_ref, nu_ref, x_ref, wg_ref, wu_ref, wd_ref, y_ref, xb_scr, acc_scr):
    i = pl.program_id(0)
    j = pl.program_id(1)

    @pl.when((i >= nu_ref[0]) & (j == 0))
    def _():
        y_ref[...] = jnp.zeros_like(y_ref)

    @pl.when(i < nu_ref[0])
    def _():
        @pl.when(j == 0)
        def _():
            xb_scr[...] = x_ref[...].astype(BF16)
            acc_scr[...] = jnp.zeros_like(acc_scr)

        x = xb_scr[...]
        a = (_silu(_dot(x, wg_ref[0])) * _dot(x, wu_ref[0])).astype(BF16)
        acc_scr[...] += _dot(a, wd_ref[0])

        @pl.when(j == pl.num_programs(1) - 1)
        def _():
            y_ref[...] = acc_scr[...]


def _moe(xs, wg, wu, wd, tile_e, n_used, tm):
    rows, d = xs.shape
    f = wg.shape[2]
    tf = _pick_tile(f, 512, LANES)
    nf = f // tf

    def fcol(i, j, nu):
        return jnp.where(i < nu[0], j, nf - 1)

    grid_spec = pltpu.PrefetchScalarGridSpec(
        num_scalar_prefetch=2,
        grid=(rows // tm, nf),
        in_specs=[pl.BlockSpec((tm, d), lambda i, j, te, nu: (i, 0)),
                  pl.BlockSpec((1, d, tf), lambda i, j, te, nu: (te[i], 0, fcol(i, j, nu))),
                  pl.BlockSpec((1, d, tf), lambda i, j, te, nu: (te[i], 0, fcol(i, j, nu))),
                  pl.BlockSpec((1, tf, d), lambda i, j, te, nu: (te[i], fcol(i, j, nu), 0))],
        out_specs=pl.BlockSpec((tm, d), lambda i, j, te, nu: (i, 0)),
        scratch_shapes=[pltpu.VMEM((tm, d), BF16), pltpu.VMEM((tm, d), F32)],
    )
    return pl.pallas_call(
        _moe_kernel,
        out_shape=jax.ShapeDtypeStruct((rows, d), F32),
        grid_spec=grid_spec,
        compiler_params=_params("arbitrary", "arbitrary"),
    )(tile_e, n_used, xs, wg, wu, wd)


def _combine_kernel(pos_ref, gate_ref, x1_ref, g2_ref, y_ref, o_ref, buf_scr, sem, *, tm, groups):
    def start(r, _):
        pltpu.make_async_copy(y_ref.at[pl.ds(pos_ref[0, 0, 2 * r], 1), :],
                              buf_scr.at[0, pl.ds(r, 1), :], sem).start()
        pltpu.make_async_copy(y_ref.at[pl.ds(pos_ref[0, 0, 2 * r + 1], 1), :],
                              buf_scr.at[1, pl.ds(r, 1), :], sem).start()
        return 0

    def wait(r, _):
        pltpu.make_async_copy(y_ref.at[pl.ds(0, 1), :], buf_scr.at[0, pl.ds(0, 1), :], sem).wait()
        return 0

    lax.fori_loop(0, tm, start, 0)
    lax.fori_loop(0, 2 * tm, wait, 0)
    for g in range(groups):
        rows = slice(g * GROUP, (g + 1) * GROUP)
        f = gate_ref[rows, 0:1] * buf_scr[0, rows, :] + gate_ref[rows, 1:2] * buf_scr[1, rows, :]
        o_ref[rows, :] = x1_ref[rows, :] + g2_ref[g:g + 1, :] * f


def _combine(y, pos2, gates, x1, mod_g):
    n, d = x1.shape
    tm = pos2.shape[2] // 2
    groups = tm // GROUP
    return pl.pallas_call(
        functools.partial(_combine_kernel, tm=tm, groups=groups),
        out_shape=jax.ShapeDtypeStruct((n, d), F32),
        grid=(n // tm,),
        in_specs=[pl.BlockSpec((1, 1, 2 * tm), lambda i: (i, 0, 0), memory_space=pltpu.SMEM),
                  pl.BlockSpec((tm, LANES), lambda i: (i, 0)),
                  pl.BlockSpec((tm, d), lambda i: (i, 0)),
                  pl.BlockSpec((groups, d), lambda i: (i, 5)),
                  pl.BlockSpec(memory_space=pl.ANY)],
        out_specs=pl.BlockSpec((tm, d), lambda i: (i, 0)),
        scratch_shapes=[pltpu.VMEM((2, tm, d), F32), pltpu.SemaphoreType.DMA(())],
        compiler_params=_params("arbitrary"),
    )(pos2, gates, x1, mod_g, y)


def _routed_ffn(h2, router, wg, wu, wd, x1, mod_g):
    n, d = x1.shape
    tile = _pick_tile(n, 640, GROUP)
    n_tiles = (2 * n) // tile + N_EXPERTS
    info, gates, counts = _router(h2, router)
    counts = counts[0, :N_EXPERTS]
    tiles_per = (counts + tile - 1) // tile
    ends = jnp.cumsum(tiles_per)
    offsets = ((ends - tiles_per) * tile).astype(jnp.int32)
    n_used = ends[-1:].astype(jnp.int32)
    t = jnp.minimum(jnp.arange(n_tiles, dtype=jnp.int32), n_used - 1)
    tile_e = jnp.sum(t[:, None] >= ends[None, :], axis=1).astype(jnp.int32)
    tm = _pick_tile(n, 256, GROUP)
    pos2 = (offsets[info[:, 2:4]] + info[:, 0:2]).reshape(n // tm, 1, 2 * tm)
    xs = _scatter(h2, pos2, counts, offsets, n_used, tile, n_tiles)
    y = _moe(xs, wg, wu, wd, tile_e, n_used, tile)
    return _combine(y, pos2, gates, x1, mod_g)


def _swap_halves(a):
    half = a.shape[-1] // 2
    return jnp.concatenate([a[..., half:], a[..., :half]], axis=-1)


def kernel(x_prompt, x_sample, c_prompt, c_sample, state_hgrn, cache_ckv, cache_kpe, ada_w, ada_b, norm1_w, norm2_w, w_in, hg_lower_bounds, hg_onorm_w, mla_qa_norm_w, mla_wqb, mla_kva_norm_w, mla_wkvb, mla_qn_nope_w, mla_qn_rope_w, mla_kn_nope_w, mla_kn_rope_w, mla_onorm_w, w_o, ffn_w_gate, ffn_w_up, ffn_w_down, moe_router, moe_w_gate, moe_w_up, moe_w_down):
    bp, sp, d = x_prompt.shape
    bs, ss, _ = x_sample.shape
    depth = ada_w.shape[0]
    past = cache_ckv.shape[2]
    n_p, n_s = bp * sp, bs * ss
    n = n_p + n_s
    hw = HG_HEADS * HG_DIM
    q_lora = mla_qa_norm_w.shape[1]
    kv_lora = mla_kva_norm_w.shape[1]
    assert sp % GROUP == 0 and ss % GROUP == 0 and n_p % ss == 0
    scale = float((NOPE + ROPE) ** -0.5)

    x = jnp.concatenate([x_prompt.reshape(n_p, d), x_sample.reshape(n_s, d)], axis=0)
    mod = _modulation(jnp.concatenate([c_prompt, c_sample], axis=0), ada_w, ada_b)
    mod_g = jnp.concatenate([jnp.repeat(mod[:, :bp], sp // GROUP, axis=1),
                             jnp.repeat(mod[:, bp:], ss // GROUP, axis=1)], axis=1)

    lw = jax.nn.softmax(hg_lower_bounds.astype(F32), axis=0)
    lower = jnp.clip(jnp.cumsum(lw, axis=0) - lw[0], 0.0, 1.0)

    pos = jnp.concatenate([jnp.tile(jnp.arange(sp), bp), jnp.tile(past + jnp.arange(ss), bs)])
    half = ROPE // 2
    inv_freq = ROPE_BASE ** (-jnp.arange(half, dtype=F32) / half)
    ang = pos.astype(F32)[:, None] * inv_freq[None, :]
    rope = jnp.concatenate([jnp.cos(ang), jnp.cos(ang), -jnp.sin(ang), jnp.sin(ang)], axis=-1)

    row2 = lambda a: a.reshape(1, -1)
    pair = lambda w: row2(jnp.concatenate([w, _swap_halves(w)]))
    zeros_state = jnp.zeros((bp,) + state_hgrn.shape[2:], F32)

    new_s_p, new_s_s, new_ckv, new_kpe = [], [], [], []
    for l in range(depth):
        kpe_cols = w_in[l][:, -ROPE:]
        w_in_l = jnp.concatenate([w_in[l], _swap_halves(kpe_cols)], axis=1).astype(BF16)
        wqb3 = mla_wqb[l].reshape(q_lora, MLA_HEADS, NOPE + ROPE)
        q_rope = wqb3[:, :, NOPE:]
        wqb_l = jnp.concatenate(
            [wqb3[:, :, :NOPE].reshape(q_lora, MLA_HEADS * NOPE),
             jnp.concatenate([q_rope, _swap_halves(q_rope)], axis=-1).reshape(q_lora, MLA_HEADS * LANES)],
            axis=1).astype(BF16)
        wkvb_l = mla_wkvb[l].astype(BF16)
        knw = row2(mla_kn_nope_w[l])

        hq, hk, lf, hv, hg, ckv, kpe, q_all, k_all, v_all = _front(
            x, mod_g[l], row2(norm1_w[l]), w_in_l, row2(lower[l]), row2(mla_qa_norm_w[l]), wqb_l,
            row2(mla_qn_nope_w[l]), pair(mla_qn_rope_w[l]), row2(mla_kva_norm_w[l]), wkvb_l, knw,
            pair(mla_kn_rope_w[l]), rope, scale)

        onw = row2(hg_onorm_w[l])
        o_hg = jnp.zeros((n, hw), BF16)
        o_hg, s_p = _hgrn(hq, hk, lf, hv, hg, zeros_state, onw, o_hg, 0, sp)
        o_hg, s_s = _hgrn(hq, hk, lf, hv, hg, state_hgrn[l].astype(F32), onw, o_hg, n_p, ss)

        kpe_past = jnp.pad(cache_kpe[l].reshape(bs * past, ROPE), ((0, 0), (0, LANES - ROPE)))
        k_past, v_past = _kv_up(cache_ckv[l].reshape(bs * past, kv_lora), kpe_past, wkvb_l, knw)
        o_mla = _attn_prompt(q_all, k_all, v_all, jnp.zeros((n, MLA_HEADS * VDIM), F32), bp, sp)
        o_mla = _attn_sample(q_all, k_all, v_all, k_past, v_past, o_mla, bs, ss, past, n_p)

        moe_layer = l % 2 == 1
        x1, h2 = _out_proj(o_hg, o_mla, x, mod_g[l], row2(mla_onorm_w[l]), w_o[l].astype(BF16),
                           row2(norm2_w[l]), F32 if moe_layer else BF16)
        j = l // 2
        if moe_layer:
            x = _routed_ffn(h2, moe_router[j], moe_w_gate[j].astype(BF16), moe_w_up[j].astype(BF16),
                            moe_w_down[j].astype(BF16), x1, mod_g[l])
        else:
            x = _ffn(h2, ffn_w_gate[j].astype(BF16), ffn_w_up[j].astype(BF16),
                     ffn_w_down[j].astype(BF16), x1, mod_g[l])

        new_s_p.append(s_p)
        new_s_s.append(s_s)
        new_ckv.append(ckv)
        new_kpe.append(kpe)

    ckv_all = jnp.stack(new_ckv)
    kpe_all = jnp.stack(new_kpe)
    return (x[:n_p].reshape(bp, sp, d), x[n_p:].reshape(bs, ss, d),
            jnp.stack(new_s_p), ckv_all[:, :n_p].reshape(depth, bp, sp, kv_lora),
            kpe_all[:, :n_p].reshape(depth, bp, sp, ROPE),
            jnp.stack(new_s_s), ckv_all[:, n_p:].reshape(depth, bs, ss, kv_lora),
            kpe_all[:, n_p:].reshape(depth, bs, ss, ROPE))
```

```python
import functools
import math

import numpy as np
import jax
import jax.numpy as jnp
from jax import lax
from jax.experimental import pallas as pl
from jax.experimental.pallas import tpu as pltpu

F32 = jnp.float32
BF16 = jnp.bfloat16
EPS = 1e-6
ROPE_BASE = 10000.0
GROUP = 32
LANES = 128
HG_HEADS = 4
HG_DIM = 128
MLA_HEADS = 4
NOPE = 128
ROPE = 64
VDIM = 128
QK_PAD = 256
CHUNK = 64
N_EXPERTS = 8
VMEM_LIMIT = 56 * 1024 * 1024

_NT = (((1,), (1,)), ((), ()))
_TN = (((0,), (0,)), ((), ()))


def _dot(a, b):
    return jnp.dot(a, b, preferred_element_type=F32)


def _dot_nt(a, b):
    return lax.dot_general(a, b, _NT, preferred_element_type=F32)


def _dot_tn(a, b):
    return lax.dot_general(a, b, _TN, preferred_element_type=F32)


def _rms(x, w):
    return x * lax.rsqrt(jnp.mean(x * x, axis=-1, keepdims=True) + EPS) * w


def _silu(x):
    return x * jax.nn.sigmoid(x)


def _split_bf16(x):
    hi = x.astype(BF16)
    lo = (x - hi.astype(F32)).astype(BF16)
    return hi, lo


def _pick_tile(n, target, mult):
    best = None
    for t in range(mult, min(n, target) + 1, mult):
        if n % t == 0:
            best = t
    assert best is not None, (n, target, mult)
    return best


def _params(*sem):
    return pltpu.CompilerParams(dimension_semantics=sem, vmem_limit_bytes=VMEM_LIMIT)


def _mod_kernel(c_ref, w_ref, b_ref, o_ref):
    sc = _silu(c_ref[...]).astype(BF16)
    o_ref[0] = _dot(sc, w_ref[0].astype(BF16)) + b_ref[0]


def _modulation(c_all, ada_w, ada_b):
    depth, d, n6 = ada_w.shape
    rows = c_all.shape[0]
    tn = _pick_tile(n6, 1536, LANES)
    return pl.pallas_call(
        _mod_kernel,
        out_shape=jax.ShapeDtypeStruct((depth, rows, n6), F32),
        grid=(depth, n6 // tn),
        in_specs=[
            pl.BlockSpec((rows, d), lambda l, j: (0, 0)),
            pl.BlockSpec((1, d, tn), lambda l, j: (l, 0, j)),
            pl.BlockSpec((1, 1, tn), lambda l, j: (l, 0, j)),
        ],
        out_specs=pl.BlockSpec((1, rows, tn), lambda l, j: (l, 0, j)),
        compiler_params=_params("arbitrary", "arbitrary"),
    )(c_all, ada_w, ada_b.reshape(depth, 1, n6))


def _rope128(g, w128, table):
    t = _rms(g, w128) * table
    return t + pltpu.roll(t, ROPE, 1)


def _front_kernel(x_ref, mod_ref, n1w_ref, win_ref, lb_ref, qaw_ref, wqb_ref, qnw_ref, qrw_ref,
                  kvaw_ref, wkvb_ref, knw_ref, krw_ref, rope_ref,
                  hq_ref, hk_ref, lf_ref, hv_ref, hg_ref, ckv_ref, kpe_ref, q_ref, k_ref, v_ref,
                  h_scr, *, groups, q_lora, kv_lora, scale):
    d = x_ref.shape[1]
    hw = HG_HEADS * HG_DIM
    n1w = n1w_ref[...]
    for g in range(groups):
        rows = slice(g * GROUP, (g + 1) * GROUP)
        shift = mod_ref[g:g + 1, 0:d]
        gain = mod_ref[g:g + 1, d:2 * d]
        h_scr[rows, :] = (_rms(x_ref[rows, :], n1w) * (1.0 + gain) + shift).astype(BF16)
    h = h_scr[...]

    hq_ref[...] = _silu(_dot(h, win_ref[:, 0:hw]))
    one_minus_f = (1.0 - lb_ref[...]) * jax.nn.sigmoid(-_dot(h, win_ref[:, hw:2 * hw]))
    hk_ref[...] = one_minus_f
    lf_ref[...] = jnp.log1p(-one_minus_f)
    hv_ref[...] = _dot(h, win_ref[:, 2 * hw:3 * hw])
    hg_ref[...] = _silu(_dot(h, win_ref[:, 3 * hw:4 * hw]))

    o = 4 * hw
    table = rope_ref[...]
    cq = _rms(_dot(h, win_ref[:, o:o + q_lora]), qaw_ref[...]).astype(BF16)
    qh = _dot(cq, wqb_ref[...])
    nw = MLA_HEADS * NOPE
    for hd in range(MLA_HEADS):
        qn = _rms(qh[:, hd * NOPE:(hd + 1) * NOPE], qnw_ref[...]) * scale
        qr = _rope128(qh[:, nw + hd * LANES:nw + (hd + 1) * LANES], qrw_ref[...], table) * scale
        q_ref[:, hd * QK_PAD:hd * QK_PAD + NOPE] = qn.astype(BF16)
        q_ref[:, hd * QK_PAD + NOPE:(hd + 1) * QK_PAD] = qr.astype(BF16)

    o += q_lora
    ckv = _rms(_dot(h, win_ref[:, o:o + kv_lora]), kvaw_ref[...])
    ckv_ref[...] = ckv
    o += kv_lora
    kp = _rope128(_dot(h, win_ref[:, o:o + LANES]), krw_ref[...], table)
    kpe_ref[...] = kp[:, 0:ROPE]
    lane = lax.broadcasted_iota(jnp.int32, kp.shape, 1)
    kp_pad = jnp.where(lane < ROPE, kp, 0.0).astype(BF16)
    kv = _dot(ckv.astype(BF16), wkvb_ref[...])
    for hd in range(MLA_HEADS):
        c0 = hd * (NOPE + VDIM)
        k_ref[:, hd * QK_PAD:hd * QK_PAD + NOPE] = _rms(kv[:, c0:c0 + NOPE], knw_ref[...]).astype(BF16)
        k_ref[:, hd * QK_PAD + NOPE:(hd + 1) * QK_PAD] = kp_pad
        v_ref[:, hd * VDIM:(hd + 1) * VDIM] = kv[:, c0 + NOPE:c0 + NOPE + VDIM].astype(BF16)


def _front(x, mod_g, n1w, w_in, lb, qaw, wqb, qnw, qrw, kvaw, wkvb, knw, krw, rope, scale):
    n, d = x.shape
    tm = _pick_tile(n, 256, GROUP)
    groups = tm // GROUP
    hw = HG_HEADS * HG_DIM
    q_lora, kv_lora = qaw.shape[1], kvaw.shape[1]
    row = lambda i: (i, 0)
    fixed = lambda i: (0, 0)

    def full(a):
        return pl.BlockSpec(a.shape, fixed)

    f32_out = lambda w: jax.ShapeDtypeStruct((n, w), F32)
    out_shape = [f32_out(hw)] * 5 + [f32_out(kv_lora), f32_out(ROPE),
                                     jax.ShapeDtypeStruct((n, MLA_HEADS * QK_PAD), BF16),
                                     jax.ShapeDtypeStruct((n, MLA_HEADS * QK_PAD), BF16),
                                     jax.ShapeDtypeStruct((n, MLA_HEADS * VDIM), BF16)]
    out_specs = [pl.BlockSpec((tm, s.shape[1]), row) for s in out_shape]
    kern = functools.partial(_front_kernel, groups=groups, q_lora=q_lora, kv_lora=kv_lora, scale=scale)
    return pl.pallas_call(
        kern,
        out_shape=out_shape,
        grid=(n // tm,),
        in_specs=[
            pl.BlockSpec((tm, d), row),
            pl.BlockSpec((groups, 2 * d), row),
            full(n1w), full(w_in), full(lb), full(qaw), full(wqb), full(qnw), full(qrw),
            full(kvaw), full(wkvb), full(knw), full(krw),
            pl.BlockSpec((tm, LANES), row),
        ],
        out_specs=out_specs,
        scratch_shapes=[pltpu.VMEM((tm, d), BF16)],
        compiler_params=_params("arbitrary"),
    )(x, mod_g, n1w, w_in, lb, qaw, wqb, qnw, qrw, kvaw, wkvb, knw, krw, rope)


def _kvup_kernel(ckv_ref, kpe_ref, wkvb_ref, knw_ref, k_ref, v_ref):
    kv = _dot(ckv_ref[...].astype(BF16), wkvb_ref[...])
    kp = kpe_ref[...].astype(BF16)
    for hd in range(MLA_HEADS):
        c0 = hd * (NOPE + VDIM)
        k_ref[:, hd * QK_PAD:hd * QK_PAD + NOPE] = _rms(kv[:, c0:c0 + NOPE], knw_ref[...]).astype(BF16)
        k_ref[:, hd * QK_PAD + NOPE:(hd + 1) * QK_PAD] = kp
        v_ref[:, hd * VDIM:(hd + 1) * VDIM] = kv[:, c0 + NOPE:c0 + NOPE + VDIM].astype(BF16)


def _kv_up(ckv, kpe_pad, wkvb, knw):
    n, r = ckv.shape
    tm = _pick_tile(n, 512, 16)
    row = lambda i: (i, 0)
    fixed = lambda i: (0, 0)
    return pl.pallas_call(
        _kvup_kernel,
        out_shape=[jax.ShapeDtypeStruct((n, MLA_HEADS * QK_PAD), BF16),
                   jax.ShapeDtypeStruct((n, MLA_HEADS * VDIM), BF16)],
        grid=(n // tm,),
        in_specs=[pl.BlockSpec((tm, r), row), pl.BlockSpec((tm, LANES), row),
                  pl.BlockSpec(wkvb.shape, fixed), pl.BlockSpec(knw.shape, fixed)],
        out_specs=[pl.BlockSpec((tm, MLA_HEADS * QK_PAD), row), pl.BlockSpec((tm, MLA_HEADS * VDIM), row)],
        compiler_params=_params("arbitrary"),
    )(ckv, kpe_pad, wkvb, knw)


def _hgrn_consts(length):
    levels = int(math.log2(length))
    assert 1 << levels == length
    expo = np.zeros(((levels + 2) * length, length), np.float32)
    idx = np.arange(length)
    for t in range(length):
        expo[t, :t + 1] = 1.0
        expo[length + t, t + 1:] = 1.0
    masks = np.zeros((levels + 1, length, length), np.float32)
    xor = idx[:, None] ^ idx[None, :]
    for m in range(levels):
        block = length >> m
        half = block >> 1
        for t in range(length):
            mid = (t // block) * block + half - 1
            if t > mid:
                expo[(2 + m) * length + t, mid + 1:t + 1] = 1.0
            else:
                expo[(2 + m) * length + t, t + 1:mid + 1] = 1.0
        masks[m] = (idx[:, None] > idx[None, :]) & (xor >= half) & (xor < block)
    masks[levels] = np.eye(length)
    return jnp.asarray(expo, BF16), jnp.asarray(masks, F32)


def _hgrn_kernel(q_ref, k_ref, lf_ref, v_ref, g_ref, s0_ref, expo_ref, mask_ref, onw_ref, alias_ref,
                 o_ref, s_ref, st_scr, *, length, levels):
    del alias_ref
    c = pl.program_id(1)

    @pl.when(c == 0)
    def _():
        for hd in range(HG_HEADS):
            st_scr[hd] = s0_ref[0, hd].T

    row_idx = lax.broadcasted_iota(jnp.int32, (length, HG_DIM), 0)
    expo = expo_ref[...]
    for hd in range(HG_HEADS):
        cols = slice(hd * HG_DIM, (hd + 1) * HG_DIM)
        lf_hi, lf_lo = _split_bf16(lf_ref[:, cols])
        decay = jnp.exp(_dot(expo, lf_hi) + _dot(expo, lf_lo))
        q = q_ref[:, cols]
        k = k_ref[:, cols]
        vb = v_ref[:, cols].astype(BF16)
        st = st_scr[hd]
        o = _dot_nt((q * decay[0:length]).astype(BF16), st.astype(BF16))
        scores = _dot_nt(q.astype(BF16), k.astype(BF16)) * mask_ref[levels]
        for m in range(levels):
            half = length >> (m + 1)
            side = jnp.where((row_idx & half) != 0, q, k)
            xm = (side * decay[(2 + m) * length:(3 + m) * length]).astype(BF16)
            scores = scores + _dot_nt(xm, xm) * mask_ref[m]
        o = o + _dot(scores.astype(BF16), vb)
        k_dec = (k * decay[length:2 * length]).astype(BF16)
        st_scr[hd] = st * decay[length - 1:length] + _dot_tn(vb, k_dec)
        o_ref[:, cols] = (_rms(o, onw_ref[...]) * g_ref[:, cols]).astype(BF16)

    @pl.when(c == pl.num_programs(1) - 1)
    def _():
        for hd in range(HG_HEADS):
            s_ref[0, hd] = st_scr[hd].T


def _hgrn(hq, hk, lf, hv, hg, s0, onw, o_alias, row0, seq):
    n, hw = hq.shape
    b = s0.shape[0]
    length = 1
    while length * 2 <= min(seq, 128) and seq % (length * 2) == 0:
        length *= 2
    levels = int(math.log2(length))
    nc = seq // length
    assert row0 % length == 0
    blk0 = row0 // length
    expo, masks = _hgrn_consts(length)
    row = lambda i, c: (blk0 + i * nc + c, 0)
    rows = pl.BlockSpec((length, hw), row)
    state = pl.BlockSpec((1, HG_HEADS, HG_DIM, HG_DIM), lambda i, c: (i, 0, 0, 0))
    kern = functools.partial(_hgrn_kernel, length=length, levels=levels)
    return pl.pallas_call(
        kern,
        out_shape=[jax.ShapeDtypeStruct((n, hw), BF16), jax.ShapeDtypeStruct(s0.shape, F32)],
        grid=(b, nc),
        in_specs=[rows, rows, rows, rows, rows, state,
                  pl.BlockSpec(expo.shape, lambda i, c: (0, 0)),
                  pl.BlockSpec(masks.shape, lambda i, c: (0, 0, 0)),
                  pl.BlockSpec(onw.shape, lambda i, c: (0, 0)),
                  pl.BlockSpec(memory_space=pl.ANY)],
        out_specs=[rows, state],
        scratch_shapes=[pltpu.VMEM((HG_HEADS, HG_DIM, HG_DIM), F32)],
        input_output_aliases={9: 0},
        compiler_params=_params("arbitrary", "arbitrary"),
    )(hq, hk, lf, hv, hg, s0, expo, masks, onw, o_alias)


def _attn_prompt_kernel(q_ref, k_ref, v_ref, alias_ref, o_ref, *, tq, nq):
    del alias_ref
    qc = lax.broadcasted_iota(jnp.int32, (tq, tq), 0) // CHUNK
    kc = lax.broadcasted_iota(jnp.int32, (tq, tq), 1) // CHUNK
    visible = kc <= qc
    for i in range(nq):
        own = slice(i * tq, (i + 1) * tq)
        q = q_ref[own, :]
        sd = jnp.where(visible, _dot_nt(q, k_ref[own, :]), -1e30)
        m = jnp.max(sd, axis=-1, keepdims=True)
        if i > 0:
            sp = _dot_nt(q, k_ref[0:i * tq, :])
            m = jnp.maximum(m, jnp.max(sp, axis=-1, keepdims=True))
        pd = jnp.exp(sd - m)
        l = jnp.sum(pd, axis=-1, keepdims=True)
        acc = _dot(pd.astype(BF16), v_ref[own, :])
        if i > 0:
            pp = jnp.exp(sp - m)
            l = l + jnp.sum(pp, axis=-1, keepdims=True)
            acc = acc + _dot(pp.astype(BF16), v_ref[0:i * tq, :])
        o_ref[own, :] = acc / l


def _attn_prompt(q_all, k_all, v_all, o_alias, b, seq):
    n = q_all.shape[0]
    tq = _pick_tile(seq, 256, CHUNK)
    nq = seq // tq
    kern = functools.partial(_attn_prompt_kernel, tq=tq, nq=nq)
    per_seq = lambda w: pl.BlockSpec((seq, w), lambda bi, h: (bi, h))
    return pl.pallas_call(
        kern,
        out_shape=jax.ShapeDtypeStruct((n, MLA_HEADS * VDIM), F32),
        grid=(b, MLA_HEADS),
        in_specs=[per_seq(QK_PAD), per_seq(QK_PAD), per_seq(VDIM), pl.BlockSpec(memory_space=pl.ANY)],
        out_specs=per_seq(VDIM),
        input_output_aliases={3: 0},
        compiler_params=_params("arbitrary", "arbitrary"),
    )(q_all, k_all, v_all, o_alias)


def _attn_sample_kernel(q_ref, kp_ref, vp_ref, kn_ref, vn_ref, alias_ref, o_ref):
    del alias_ref
    q = q_ref[...]
    sp = _dot_nt(q, kp_ref[...])
    sn = _dot_nt(q, kn_ref[...])
    m = jnp.maximum(jnp.max(sp, axis=-1, keepdims=True), jnp.max(sn, axis=-1, keepdims=True))
    pp = jnp.exp(sp - m)
    pn = jnp.exp(sn - m)
    l = jnp.sum(pp, axis=-1, keepdims=True) + jnp.sum(pn, axis=-1, keepdims=True)
    acc = _dot(pp.astype(BF16), vp_ref[...]) + _dot(pn.astype(BF16), vn_ref[...])
    o_ref[...] = acc / l


def _attn_sample(q_all, k_all, v_all, k_past, v_past, o_alias, b, seq, past, row0):
    assert row0 % seq == 0
    blk0 = row0 // seq
    new_q = lambda bi, h: (blk0 + bi, h)
    return pl.pallas_call(
        _attn_sample_kernel,
        out_shape=jax.ShapeDtypeStruct(o_alias.shape, F32),
        grid=(b, MLA_HEADS),
        in_specs=[pl.BlockSpec((seq, QK_PAD), new_q),
                  pl.BlockSpec((past, QK_PAD), lambda bi, h: (bi, h)),
                  pl.BlockSpec((past, VDIM), lambda bi, h: (bi, h)),
                  pl.BlockSpec((seq, QK_PAD), new_q),
                  pl.BlockSpec((seq, VDIM), new_q),
                  pl.BlockSpec(memory_space=pl.ANY)],
        out_specs=pl.BlockSpec((seq, VDIM), new_q),
        input_output_aliases={5: 0},
        compiler_params=_params("arbitrary", "arbitrary"),
    )(q_all, k_past, v_past, k_all, v_all, o_alias)


def _out_kernel(ohg_ref, omla_ref, x_ref, g1_ref, sh2_ref, sc2_ref, monw_ref, wo_ref, n2w_ref,
                x1_ref, h2_ref, *, groups):
    hw = ohg_ref.shape[1]
    om = _rms(omla_ref[...], monw_ref[...]).astype(BF16)
    y = _dot(ohg_ref[...], wo_ref[0:hw, :]) + _dot(om, wo_ref[hw:, :])
    n2w = n2w_ref[...]
    for g in range(groups):
        rows = slice(g * GROUP, (g + 1) * GROUP)
        x1 = x_ref[rows, :] + g1_ref[g:g + 1, :] * y[rows, :]
        x1_ref[rows, :] = x1
        h2 = _rms(x1, n2w) * (1.0 + sc2_ref[g:g + 1, :]) + sh2_ref[g:g + 1, :]
        h2_ref[rows, :] = h2.astype(h2_ref.dtype)


def _out_proj(o_hg, o_mla, x, mod_g, monw, w_o, n2w, h2_dtype):
    n, d = x.shape
    tm = _pick_tile(n, 256, GROUP)
    groups = tm // GROUP
    row = lambda i: (i, 0)
    fixed = lambda i: (0, 0)
    modspec = lambda col: pl.BlockSpec((groups, d), lambda i: (i, col))
    return pl.pallas_call(
        functools.partial(_out_kernel, groups=groups),
        out_shape=[jax.ShapeDtypeStruct((n, d), F32), jax.ShapeDtypeStruct((n, d), h2_dtype)],
        grid=(n // tm,),
        in_specs=[pl.BlockSpec((tm, o_hg.shape[1]), row), pl.BlockSpec((tm, o_mla.shape[1]), row),
                  pl.BlockSpec((tm, d), row), modspec(2), modspec(3), modspec(4),
                  pl.BlockSpec(monw.shape, fixed), pl.BlockSpec(w_o.shape, fixed),
                  pl.BlockSpec(n2w.shape, fixed)],
        out_specs=[pl.BlockSpec((tm, d), row), pl.BlockSpec((tm, d), row)],
        compiler_params=_params("arbitrary"),
    )(o_hg, o_mla, x, mod_g, mod_g, mod_g, monw, w_o, n2w)


def _ffn_kernel(h_ref, wg_ref, wu_ref, wd_ref, x1_ref, g2_ref, o_ref, acc_scr, *, groups):
    j = pl.program_id(1)

    @pl.when(j == 0)
    def _():
        acc_scr[...] = jnp.zeros_like(acc_scr)

    h = h_ref[...]
    a = (_silu(_dot(h, wg_ref[...])) * _dot(h, wu_ref[...])).astype(BF16)
    acc_scr[...] += _dot(a, wd_ref[...])

    @pl.when(j == pl.num_programs(1) - 1)
    def _():
        for g in range(groups):
            rows = slice(g * GROUP, (g + 1) * GROUP)
            o_ref[rows, :] = x1_ref[rows, :] + g2_ref[g:g + 1, :] * acc_scr[rows, :]


def _ffn(h2, wg, wu, wd, x1, mod_g):
    n, d = x1.shape
    f = wg.shape[1]
    tm = _pick_tile(n, 1280, GROUP)
    tf = _pick_tile(f, 512, LANES)
    groups = tm // GROUP
    return pl.pallas_call(
        functools.partial(_ffn_kernel, groups=groups),
        out_shape=jax.ShapeDtypeStruct((n, d), F32),
        grid=(n // tm, f // tf),
        in_specs=[pl.BlockSpec((tm, d), lambda i, j: (i, 0)),
                  pl.BlockSpec((d, tf), lambda i, j: (0, j)),
                  pl.BlockSpec((d, tf), lambda i, j: (0, j)),
                  pl.BlockSpec((tf, d), lambda i, j: (j, 0)),
                  pl.BlockSpec((tm, d), lambda i, j: (i, 0)),
                  pl.BlockSpec((groups, d), lambda i, j: (i, 5))],
        out_specs=pl.BlockSpec((tm, d), lambda i, j: (i, 0)),
        scratch_shapes=[pltpu.VMEM((tm, d), F32)],
        compiler_params=_params("arbitrary", "arbitrary"),
    )(h2, wg, wu, wd, x1, mod_g)


def _router_kernel(h_ref, rhi_ref, rlo_ref, tril_ref, pos_ref, gate_ref, cnt_ref, cnt_scr):
    i = pl.program_id(0)

    @pl.when(i == 0)
    def _():
        cnt_scr[...] = jnp.zeros_like(cnt_scr)

    h_hi, h_lo = _split_bf16(h_ref[...])
    logits = _dot(h_hi, rhi_ref[...]) + (_dot(h_hi, rlo_ref[...]) + _dot(h_lo, rhi_ref[...]))
    lane = lax.broadcasted_iota(jnp.int32, logits.shape, 1).astype(F32)
    logits = jnp.where(lane < N_EXPERTS, logits, -1e30)
    m1 = jnp.max(logits, axis=-1, keepdims=True)
    i1 = jnp.min(jnp.where(logits == m1, lane, float(LANES)), axis=-1, keepdims=True)
    hot1 = lane == i1
    rest = jnp.where(hot1, -1e30, logits)
    m2 = jnp.max(rest, axis=-1, keepdims=True)
    i2 = jnp.min(jnp.where(rest == m2, lane, float(LANES)), axis=-1, keepdims=True)
    hot2 = lane == i2
    e = jnp.exp(m2 - m1)
    gate1 = 1.0 / (1.0 + e)
    gate2 = e / (1.0 + e)
    gate_ref[...] = jnp.where(lane == 0, gate1, jnp.where(lane == 1, gate2, 0.0))
    hot = jnp.where(hot1, 1.0, jnp.where(hot2, 1.0, 0.0))
    before = _dot(tril_ref[...], hot.astype(BF16)) + cnt_scr[0:1, :]
    rank1 = jnp.sum(jnp.where(hot1, before, 0.0), axis=-1, keepdims=True)
    rank2 = jnp.sum(jnp.where(hot2, before, 0.0), axis=-1, keepdims=True)
    info = jnp.where(lane == 0, rank1, jnp.where(lane == 1, rank2,
                     jnp.where(lane == 2, i1, jnp.where(lane == 3, i2, 0.0))))
    pos_ref[...] = info.astype(jnp.int32)
    total = cnt_scr[0:1, :] + jnp.sum(hot, axis=0, keepdims=True)
    cnt_scr[...] = jnp.broadcast_to(total, cnt_scr.shape)
    cnt_ref[...] = jnp.broadcast_to(total, cnt_ref.shape).astype(jnp.int32)


def _router(h2, router):
    n, d = h2.shape
    tm = _pick_tile(n, 256, GROUP)
    r_pad = jnp.zeros((d, LANES), F32).at[:, :N_EXPERTS].set(router)
    r_hi = r_pad.astype(BF16)
    r_lo = (r_pad - r_hi.astype(F32)).astype(BF16)
    tril = jnp.asarray(np.tril(np.ones((tm, tm), np.float32), -1), BF16)
    row = lambda i: (i, 0)
    fixed = lambda i: (0, 0)
    return pl.pallas_call(
        _router_kernel,
        out_shape=[jax.ShapeDtypeStruct((n, LANES), jnp.int32), jax.ShapeDtypeStruct((n, LANES), F32),
                   jax.ShapeDtypeStruct((8, LANES), jnp.int32)],
        grid=(n // tm,),
        in_specs=[pl.BlockSpec((tm, d), row), pl.BlockSpec((d, LANES), fixed),
                  pl.BlockSpec((d, LANES), fixed), pl.BlockSpec((tm, tm), fixed)],
        out_specs=[pl.BlockSpec((tm, LANES), row), pl.BlockSpec((tm, LANES), row),
                   pl.BlockSpec((8, LANES), fixed)],
        scratch_shapes=[pltpu.VMEM((8, LANES), F32)],
        compiler_params=_params("arbitrary"),
    )(h2, r_hi, r_lo, tril)


def _row_copy(src, src_row, dst, dst_row, sem):
    return pltpu.make_async_copy(src.at[pl.ds(src_row, 1), :], dst.at[pl.ds(dst_row, 1), :], sem)


def _scatter_kernel(cnt_ref, off_ref, nu_ref, pos_ref, h_ref, xs_ref, zero_scr, sem, *, tm, tile, n_tiles):
    i = pl.program_id(0)

    def start(r, _):
        _row_copy(h_ref, r, xs_ref, pos_ref[0, 0, 2 * r], sem).start()
        _row_copy(h_ref, r, xs_ref, pos_ref[0, 0, 2 * r + 1], sem).start()
        return 0

    lax.fori_loop(0, tm, start, 0, unroll=8)
    for _ in range(2 * tm):
        _row_copy(h_ref, 0, xs_ref, 0, sem).wait()

    @pl.when(i == pl.num_programs(0) - 1)
    def _():
        zero_scr[...] = jnp.zeros_like(zero_scr)
        for e in range(N_EXPERTS):
            cnt = cnt_ref[e]
            end = ((cnt + tile - 1) // tile) * tile

            def fill(r, _):
                _row_copy(zero_scr, 0, xs_ref, off_ref[e] + r, sem).start()
                return 0

            def fill_wait(r, _):
                _row_copy(zero_scr, 0, xs_ref, 0, sem).wait()
                return 0

            lax.fori_loop(cnt, end, fill, 0)
            lax.fori_loop(cnt, end, fill_wait, 0)

        def fill_tile(t, _):
            cp = pltpu.make_async_copy(zero_scr, xs_ref.at[pl.ds(pl.multiple_of(t * tile, tile), tile), :], sem)
            cp.start()
            cp.wait()
            return 0

        lax.fori_loop(nu_ref[0], n_tiles, fill_tile, 0)


def _scatter(h2, pos2, counts, offsets, n_used, tile, n_tiles):
    n, d = h2.shape
    tm = pos2.shape[2] // 2
    grid_spec = pltpu.PrefetchScalarGridSpec(
        num_scalar_prefetch=3,
        grid=(n // tm,),
        in_specs=[pl.BlockSpec((1, 1, 2 * tm), lambda i, *_: (i, 0, 0), memory_space=pltpu.SMEM),
                  pl.BlockSpec((tm, d), lambda i, *_: (i, 0))],
        out_specs=pl.BlockSpec(memory_space=pl.ANY),
        scratch_shapes=[pltpu.VMEM((tile, d), F32), pltpu.SemaphoreType.DMA(())],
    )
    return pl.pallas_call(
        functools.partial(_scatter_kernel, tm=tm, tile=tile, n_tiles=n_tiles),
        out_shape=jax.ShapeDtypeStruct((n_tiles * tile, d), F32),
        grid_spec=grid_spec,
        compiler_params=_params("arbitrary"),
    )(counts, offsets, n_used, pos2, h2)


def _moe_kernel(te_ref, nu_ref, x_ref, wg_ref, wu_ref, wd_ref, y_ref, xb_scr, acc_scr):
    i = pl.program_id(0)
    j = pl.program_id(1)

    @pl.when((i >= nu_ref[0]) & (j == 0))
    def _():
        y_ref[...] = jnp.zeros_like(y_ref)

    @pl.when(i < nu_ref[0])
    def _():
        @pl.when(j == 0)
        def _():
            xb_scr[...] = x_ref[...].astype(BF16)
            acc_scr[...] = jnp.zeros_like(acc_scr)

        x = xb_scr[...]
        a = (_silu(_dot(x, wg_ref[0])) * _dot(x, wu_ref[0])).astype(BF16)
        acc_scr[...] += _dot(a, wd_ref[0])

        @pl.when(j == pl.num_programs(1) - 1)
        def _():
            y_ref[...] = acc_scr[...]


def _moe(xs, wg, wu, wd, tile_e, n_used, tm):
    rows, d = xs.shape
    f = wg.shape[2]
    tf = _pick_tile(f, 512, LANES)
    nf = f // tf

    def fcol(i, j, nu):
        return jnp.where(i < nu[0], j, nf - 1)

    grid_spec = pltpu.PrefetchScalarGridSpec(
        num_scalar_prefetch=2,
        grid=(rows // tm, nf),
        in_specs=[pl.BlockSpec((tm, d), lambda i, j, te, nu: (i, 0)),
                  pl.BlockSpec((1, d, tf), lambda i, j, te, nu: (te[i], 0, fcol(i, j, nu))),
                  pl.BlockSpec((1, d, tf), lambda i, j, te, nu: (te[i], 0, fcol(i, j, nu))),
                  pl.BlockSpec((1, tf, d), lambda i, j, te, nu: (te[i], fcol(i, j, nu), 0))],
        out_specs=pl.BlockSpec((tm, d), lambda i, j, te, nu: (i, 0)),
        scratch_shapes=[pltpu.VMEM((tm, d), BF16), pltpu.VMEM((tm, d), F32)],
    )
    return pl.pallas_call(
        _moe_kernel,
        out_shape=jax.ShapeDtypeStruct((rows, d), F32),
        grid_spec=grid_spec,
        compiler_params=_params("arbitrary", "arbitrary"),
    )(tile_e, n_used, xs, wg, wu, wd)


def _combine_kernel(pos_ref, gate_ref, x1_ref, g2_ref, y_ref, o_ref, buf_scr, sem, *, tm, groups):
    def start(r, _):
        pltpu.make_async_copy(y_ref.at[pl.ds(pos_ref[0, 0, 2 * r], 1), :],
                              buf_scr.at[0, pl.ds(r, 1), :], sem).start()
        pltpu.make_async_copy(y_ref.at[pl.ds(pos_ref[0, 0, 2 * r + 1], 1), :],
                              buf_scr.at[1, pl.ds(r, 1), :], sem).start()
        return 0

    lax.fori_loop(0, tm, start, 0, unroll=8)
    for _ in range(2 * tm):
        pltpu.make_async_copy(y_ref.at[pl.ds(0, 1), :], buf_scr.at[0, pl.ds(0, 1), :], sem).wait()
    for g in range(groups):
        rows = slice(g * GROUP, (g + 1) * GROUP)
        f = gate_ref[rows, 0:1] * buf_scr[0, rows, :] + gate_ref[rows, 1:2] * buf_scr[1, rows, :]
        o_ref[rows, :] = x1_ref[rows, :] + g2_ref[g:g + 1, :] * f


def _combine(y, pos2, gates, x1, mod_g):
    n, d = x1.shape
    tm = pos2.shape[2] // 2
    groups = tm // GROUP
    return pl.pallas_call(
        functools.partial(_combine_kernel, tm=tm, groups=groups),
        out_shape=jax.ShapeDtypeStruct((n, d), F32),
        grid=(n // tm,),
        in_specs=[pl.BlockSpec((1, 1, 2 * tm), lambda i: (i, 0, 0), memory_space=pltpu.SMEM),
                  pl.BlockSpec((tm, LANES), lambda i: (i, 0)),
                  pl.BlockSpec((tm, d), lambda i: (i, 0)),
                  pl.BlockSpec((groups, d), lambda i: (i, 5)),
                  pl.BlockSpec(memory_space=pl.ANY)],
        out_specs=pl.BlockSpec((tm, d), lambda i: (i, 0)),
        scratch_shapes=[pltpu.VMEM((2, tm, d), F32), pltpu.SemaphoreType.DMA(())],
        compiler_params=_params("arbitrary"),
    )(pos2, gates, x1, mod_g, y)


def _routed_ffn(h2, router, wg, wu, wd, x1, mod_g):
    n, d = x1.shape
    tile = _pick_tile(n, 640, GROUP)
    n_tiles = (2 * n) // tile + N_EXPERTS
    info, gates, counts = _router(h2, router)
    counts = counts[0, :N_EXPERTS]
    tiles_per = (counts + tile - 1) // tile
    ends = jnp.cumsum(tiles_per)
    offsets = ((ends - tiles_per) * tile).astype(jnp.int32)
    n_used = ends[-1:].astype(jnp.int32)
    t = jnp.minimum(jnp.arange(n_tiles, dtype=jnp.int32), n_used - 1)
    tile_e = jnp.sum(t[:, None] >= ends[None, :], axis=1).astype(jnp.int32)
    tm = _pick_tile(n, 256, GROUP)
    pos2 = (offsets[info[:, 2:4]] + info[:, 0:2]).reshape(n // tm, 1, 2 * tm)
    xs = _scatter(h2, pos2, counts, offsets, n_used, tile, n_tiles)
    y = _moe(xs, wg, wu, wd, tile_e, n_used, tile)
    return _combine(y, pos2, gates, x1, mod_g)


def _swap_halves(a):
    half = a.shape[-1] // 2
    return jnp.concatenate([a[..., half:], a[..., :half]], axis=-1)


def kernel(x_prompt, x_sample, c_prompt, c_sample, state_hgrn, cache_ckv, cache_kpe, ada_w, ada_b, norm1_w, norm2_w, w_in, hg_lower_bounds, hg_onorm_w, mla_qa_norm_w, mla_wqb, mla_kva_norm_w, mla_wkvb, mla_qn_nope_w, mla_qn_rope_w, mla_kn_nope_w, mla_kn_rope_w, mla_onorm_w, w_o, ffn_w_gate, ffn_w_up, ffn_w_down, moe_router, moe_w_gate, moe_w_up, moe_w_down):
    bp, sp, d = x_prompt.shape
    bs, ss, _ = x_sample.shape
    depth = ada_w.shape[0]
    past = cache_ckv.shape[2]
    n_p, n_s = bp * sp, bs * ss
    n = n_p + n_s
    hw = HG_HEADS * HG_DIM
    q_lora = mla_qa_norm_w.shape[1]
    kv_lora = mla_kva_norm_w.shape[1]
    assert sp % GROUP == 0 and ss % GROUP == 0 and n_p % ss == 0
    scale = float((NOPE + ROPE) ** -0.5)

    x = jnp.concatenate([x_prompt.reshape(n_p, d), x_sample.reshape(n_s, d)], axis=0)
    mod = _modulation(jnp.concatenate([c_prompt, c_sample], axis=0), ada_w, ada_b)
    mod_g = jnp.concatenate([jnp.repeat(mod[:, :bp], sp // GROUP, axis=1),
                             jnp.repeat(mod[:, bp:], ss // GROUP, axis=1)], axis=1)

    lw = jax.nn.softmax(hg_lower_bounds.astype(F32), axis=0)
    lower = jnp.clip(jnp.cumsum(lw, axis=0) - lw[0], 0.0, 1.0)

    pos = jnp.concatenate([jnp.tile(jnp.arange(sp), bp), jnp.tile(past + jnp.arange(ss), bs)])
    half = ROPE // 2
    inv_freq = ROPE_BASE ** (-jnp.arange(half, dtype=F32) / half)
    ang = pos.astype(F32)[:, None] * inv_freq[None, :]
    rope = jnp.concatenate([jnp.cos(ang), jnp.cos(ang), -jnp.sin(ang), jnp.sin(ang)], axis=-1)

    row2 = lambda a: a.reshape(1, -1)
    pair = lambda w: row2(jnp.concatenate([w, _swap_halves(w)]))
    zeros_state = jnp.zeros((bp,) + state_hgrn.shape[2:], F32)

    new_s_p, new_s_s, new_ckv, new_kpe = [], [], [], []
    for l in range(depth):
        kpe_cols = w_in[l][:, -ROPE:]
        w_in_l = jnp.concatenate([w_in[l], _swap_halves(kpe_cols)], axis=1).astype(BF16)
        wqb3 = mla_wqb[l].reshape(q_lora, MLA_HEADS, NOPE + ROPE)
        q_rope = wqb3[:, :, NOPE:]
        wqb_l = jnp.concatenate(
            [wqb3[:, :, :NOPE].reshape(q_lora, MLA_HEADS * NOPE),
             jnp.concatenate([q_rope, _swap_halves(q_rope)], axis=-1).reshape(q_lora, MLA_HEADS * LANES)],
            axis=1).astype(BF16)
        wkvb_l = mla_wkvb[l].astype(BF16)
        knw = row2(mla_kn_nope_w[l])

        hq, hk, lf, hv, hg, ckv, kpe, q_all, k_all, v_all = _front(
            x, mod_g[l], row2(norm1_w[l]), w_in_l, row2(lower[l]), row2(mla_qa_norm_w[l]), wqb_l,
            row2(mla_qn_nope_w[l]), pair(mla_qn_rope_w[l]), row2(mla_kva_norm_w[l]), wkvb_l, knw,
            pair(mla_kn_rope_w[l]), rope, scale)

        onw = row2(hg_onorm_w[l])
        o_hg = jnp.zeros((n, hw), BF16)
        o_hg, s_p = _hgrn(hq, hk, lf, hv, hg, zeros_state, onw, o_hg, 0, sp)
        o_hg, s_s = _hgrn(hq, hk, lf, hv, hg, state_hgrn[l].astype(F32), onw, o_hg, n_p, ss)

        kpe_past = jnp.pad(cache_kpe[l].reshape(bs * past, ROPE), ((0, 0), (0, LANES - ROPE)))
        k_past, v_past = _kv_up(cache_ckv[l].reshape(bs * past, kv_lora), kpe_past, wkvb_l, knw)
        o_mla = _attn_prompt(q_all, k_all, v_all, jnp.zeros((n, MLA_HEADS * VDIM), F32), bp, sp)
        o_mla = _attn_sample(q_all, k_all, v_all, k_past, v_past, o_mla, bs, ss, past, n_p)

        moe_layer = l % 2 == 1
        x1, h2 = _out_proj(o_hg, o_mla, x, mod_g[l], row2(mla_onorm_w[l]), w_o[l].astype(BF16),
                           row2(norm2_w[l]), F32 if moe_layer else BF16)
        j = l // 2
        if moe_layer:
            x = _routed_ffn(h2, moe_router[j], moe_w_gate[j].astype(BF16), moe_w_up[j].astype(BF16),
                            moe_w_down[j].astype(BF16), x1, mod_g[l])
        else:
            x = _ffn(h2, ffn_w_gate[j].astype(BF16), ffn_w_up[j].astype(BF16),
                     ffn_w_down[j].astype(BF16), x1, mod_g[l])

        new_s_p.append(s_p)
        new_s_s.append(s_s)
        new_ckv.append(ckv)
        new_kpe.append(kpe)

    ckv_all = jnp.stack(new_ckv)
    kpe_all = jnp.stack(new_kpe)
    return (x[:n_p].reshape(bp, sp, d), x[n_p:].reshape(bs, ss, d),
            jnp.stack(new_s_p), ckv_all[:, :n_p].reshape(depth, bp, sp, kv_lora),
            kpe_all[:, :n_p].reshape(depth, bp, sp, ROPE),
            jnp.stack(new_s_s), ckv_all[:, n_p:].reshape(depth, bs, ss, kv_lora),
            kpe_all[:, n_p:].reshape(depth, bs, ss, ROPE))
```

```python
import functools
import math

import numpy as np
import jax
import jax.numpy as jnp
from jax import lax
from jax.experimental import pallas as pl
from jax.experimental.pallas import tpu as pltpu

F32 = jnp.float32
BF16 = jnp.bfloat16
EPS = 1e-6
ROPE_BASE = 10000.0
GROUP = 32
LANES = 128
HG_HEADS = 4
HG_DIM = 128
MLA_HEADS = 4
NOPE = 128
ROPE = 64
VDIM = 128
QK_PAD = 256
CHUNK = 64
N_EXPERTS = 8
VMEM_LIMIT = 56 * 1024 * 1024

_NT = (((1,), (1,)), ((), ()))
_TN = (((0,), (0,)), ((), ()))


def _dot(a, b):
    return jnp.dot(a, b, preferred_element_type=F32)


def _dot_nt(a, b):
    return lax.dot_general(a, b, _NT, preferred_element_type=F32)


def _dot_tn(a, b):
    return lax.dot_general(a, b, _TN, preferred_element_type=F32)


def _rms(x, w):
    return x * lax.rsqrt(jnp.mean(x * x, axis=-1, keepdims=True) + EPS) * w


def _silu(x):
    return x * jax.nn.sigmoid(x)


def _split_bf16(x):
    hi = x.astype(BF16)
    lo = (x - hi.astype(F32)).astype(BF16)
    return hi, lo


def _pick_tile(n, target, mult):
    best = None
    for t in range(mult, min(n, target) + 1, mult):
        if n % t == 0:
            best = t
    assert best is not None, (n, target, mult)
    return best


def _params(*sem):
    return pltpu.CompilerParams(dimension_semantics=sem, vmem_limit_bytes=VMEM_LIMIT)


def _mod_kernel(c_ref, w_ref, b_ref, o_ref):
    sc = _silu(c_ref[...]).astype(BF16)
    o_ref[0] = _dot(sc, w_ref[0].astype(BF16)) + b_ref[0]


def _modulation(c_all, ada_w, ada_b):
    depth, d, n6 = ada_w.shape
    rows = c_all.shape[0]
    tn = _pick_tile(n6, 1536, LANES)
    return pl.pallas_call(
        _mod_kernel,
        out_shape=jax.ShapeDtypeStruct((depth, rows, n6), F32),
        grid=(depth, n6 // tn),
        in_specs=[
            pl.BlockSpec((rows, d), lambda l, j: (0, 0)),
            pl.BlockSpec((1, d, tn), lambda l, j: (l, 0, j)),
            pl.BlockSpec((1, 1, tn), lambda l, j: (l, 0, j)),
        ],
        out_specs=pl.BlockSpec((1, rows, tn), lambda l, j: (l, 0, j)),
        compiler_params=_params("arbitrary", "arbitrary"),
    )(c_all, ada_w, ada_b.reshape(depth, 1, n6))


def _rope128(g, w128, table):
    t = _rms(g, w128) * table
    return t + pltpu.roll(t, ROPE, 1)


def _front_kernel(x_ref, mod_ref, n1w_ref, win_ref, lb_ref, qaw_ref, wqb_ref, qnw_ref, qrw_ref,
                  kvaw_ref, wkvb_ref, knw_ref, krw_ref, rope_ref,
                  hq_ref, hk_ref, lf_ref, hv_ref, hg_ref, ckv_ref, kpe_ref, q_ref, k_ref, v_ref,
                  h_scr, *, groups, q_lora, kv_lora, scale):
    d = x_ref.shape[1]
    hw = HG_HEADS * HG_DIM
    n1w = n1w_ref[...]
    for g in range(groups):
        rows = slice(g * GROUP, (g + 1) * GROUP)
        shift = mod_ref[g:g + 1, 0:d]
        gain = mod_ref[g:g + 1, d:2 * d]
        h_scr[rows, :] = (_rms(x_ref[rows, :], n1w) * (1.0 + gain) + shift).astype(BF16)
    h = h_scr[...]

    hq_ref[...] = _silu(_dot(h, win_ref[:, 0:hw]))
    one_minus_f = (1.0 - lb_ref[...]) * jax.nn.sigmoid(-_dot(h, win_ref[:, hw:2 * hw]))
    hk_ref[...] = one_minus_f
    lf_ref[...] = jnp.log1p(-one_minus_f)
    hv_ref[...] = _dot(h, win_ref[:, 2 * hw:3 * hw])
    hg_ref[...] = _silu(_dot(h, win_ref[:, 3 * hw:4 * hw]))

    o = 4 * hw
    table = rope_ref[...]
    cq = _rms(_dot(h, win_ref[:, o:o + q_lora]), qaw_ref[...]).astype(BF16)
    qh = _dot(cq, wqb_ref[...])
    nw = MLA_HEADS * NOPE
    for hd in range(MLA_HEADS):
        qn = _rms(qh[:, hd * NOPE:(hd + 1) * NOPE], qnw_ref[...]) * scale
        qr = _rope128(qh[:, nw + hd * LANES:nw + (hd + 1) * LANES], qrw_ref[...], table) * scale
        q_ref[:, hd * QK_PAD:hd * QK_PAD + NOPE] = qn.astype(BF16)
        q_ref[:, hd * QK_PAD + NOPE:(hd + 1) * QK_PAD] = qr.astype(BF16)

    o += q_lora
    ckv = _rms(_dot(h, win_ref[:, o:o + kv_lora]), kvaw_ref[...])
    ckv_ref[...] = ckv
    o += kv_lora
    kp = _rope128(_dot(h, win_ref[:, o:o + LANES]), krw_ref[...], table)
    kpe_ref[...] = kp[:, 0:ROPE]
    lane = lax.broadcasted_iota(jnp.int32, kp.shape, 1)
    kp_pad = jnp.where(lane < ROPE, kp, 0.0).astype(BF16)
    kv = _dot(ckv.astype(BF16), wkvb_ref[...])
    for hd in range(MLA_HEADS):
        c0 = hd * (NOPE + VDIM)
        k_ref[:, hd * QK_PAD:hd * QK_PAD + NOPE] = _rms(kv[:, c0:c0 + NOPE], knw_ref[...]).astype(BF16)
        k_ref[:, hd * QK_PAD + NOPE:(hd + 1) * QK_PAD] = kp_pad
        v_ref[:, hd * VDIM:(hd + 1) * VDIM] = kv[:, c0 + NOPE:c0 + NOPE + VDIM].astype(BF16)


def _front(x, mod_g, n1w, w_in, lb, qaw, wqb, qnw, qrw, kvaw, wkvb, knw, krw, rope, scale):
    n, d = x.shape
    tm = _pick_tile(n, 256, GROUP)
    groups = tm // GROUP
    hw = HG_HEADS * HG_DIM
    q_lora, kv_lora = qaw.shape[1], kvaw.shape[1]
    row = lambda i: (i, 0)
    fixed = lambda i: (0, 0)

    def full(a):
        return pl.BlockSpec(a.shape, fixed)

    f32_out = lambda w: jax.ShapeDtypeStruct((n, w), F32)
    out_shape = [f32_out(hw)] * 5 + [f32_out(kv_lora), f32_out(ROPE),
                                     jax.ShapeDtypeStruct((n, MLA_HEADS * QK_PAD), BF16),
                                     jax.ShapeDtypeStruct((n, MLA_HEADS * QK_PAD), BF16),
                                     jax.ShapeDtypeStruct((n, MLA_HEADS * VDIM), BF16)]
    out_specs = [pl.BlockSpec((tm, s.shape[1]), row) for s in out_shape]
    kern = functools.partial(_front_kernel, groups=groups, q_lora=q_lora, kv_lora=kv_lora, scale=scale)
    return pl.pallas_call(
        kern,
        out_shape=out_shape,
        grid=(n // tm,),
        in_specs=[
            pl.BlockSpec((tm, d), row),
            pl.BlockSpec((groups, 2 * d), row),
            full(n1w), full(w_in), full(lb), full(qaw), full(wqb), full(qnw), full(qrw),
            full(kvaw), full(wkvb), full(knw), full(krw),
            pl.BlockSpec((tm, LANES), row),
        ],
        out_specs=out_specs,
        scratch_shapes=[pltpu.VMEM((tm, d), BF16)],
        compiler_params=_params("arbitrary"),
    )(x, mod_g, n1w, w_in, lb, qaw, wqb, qnw, qrw, kvaw, wkvb, knw, krw, rope)


def _kvup_kernel(ckv_ref, kpe_ref, wkvb_ref, knw_ref, k_ref, v_ref):
    kv = _dot(ckv_ref[...].astype(BF16), wkvb_ref[...])
    kp = kpe_ref[...].astype(BF16)
    for hd in range(MLA_HEADS):
        c0 = hd * (NOPE + VDIM)
        k_ref[:, hd * QK_PAD:hd * QK_PAD + NOPE] = _rms(kv[:, c0:c0 + NOPE], knw_ref[...]).astype(BF16)
        k_ref[:, hd * QK_PAD + NOPE:(hd + 1) * QK_PAD] = kp
        v_ref[:, hd * VDIM:(hd + 1) * VDIM] = kv[:, c0 + NOPE:c0 + NOPE + VDIM].astype(BF16)


def _kv_up(ckv, kpe_pad, wkvb, knw):
    n, r = ckv.shape
    tm = _pick_tile(n, 512, 16)
    row = lambda i: (i, 0)
    fixed = lambda i: (0, 0)
    return pl.pallas_call(
        _kvup_kernel,
        out_shape=[jax.ShapeDtypeStruct((n, MLA_HEADS * QK_PAD), BF16),
                   jax.ShapeDtypeStruct((n, MLA_HEADS * VDIM), BF16)],
        grid=(n // tm,),
        in_specs=[pl.BlockSpec((tm, r), row), pl.BlockSpec((tm, LANES), row),
                  pl.BlockSpec(wkvb.shape, fixed), pl.BlockSpec(knw.shape, fixed)],
        out_specs=[pl.BlockSpec((tm, MLA_HEADS * QK_PAD), row), pl.BlockSpec((tm, MLA_HEADS * VDIM), row)],
        compiler_params=_params("arbitrary"),
    )(ckv, kpe_pad, wkvb, knw)


def _hgrn_consts(length):
    levels = int(math.log2(length))
    assert 1 << levels == length
    expo = np.zeros(((levels + 2) * length, length), np.float32)
    idx = np.arange(length)
    for t in range(length):
        expo[t, :t + 1] = 1.0
        expo[length + t, t + 1:] = 1.0
    masks = np.zeros((levels + 1, length, length), np.float32)
    xor = idx[:, None] ^ idx[None, :]
    for m in range(levels):
        block = length >> m
        half = block >> 1
        for t in range(length):
            mid = (t // block) * block + half - 1
            if t > mid:
                expo[(2 + m) * length + t, mid + 1:t + 1] = 1.0
            else:
                expo[(2 + m) * length + t, t + 1:mid + 1] = 1.0
        masks[m] = (idx[:, None] > idx[None, :]) & (xor >= half) & (xor < block)
    masks[levels] = np.eye(length)
    return jnp.asarray(expo, BF16), jnp.asarray(masks, F32)


def _hgrn_kernel(q_ref, k_ref, lf_ref, v_ref, g_ref, s0_ref, expo_ref, mask_ref, onw_ref, alias_ref,
                 o_ref, s_ref, st_scr, *, length, levels):
    del alias_ref
    c = pl.program_id(1)

    @pl.when(c == 0)
    def _():
        for hd in range(HG_HEADS):
            st_scr[hd] = s0_ref[0, hd].T

    row_idx = lax.broadcasted_iota(jnp.int32, (length, HG_DIM), 0)
    expo = expo_ref[...]
    for hd in range(HG_HEADS):
        cols = slice(hd * HG_DIM, (hd + 1) * HG_DIM)
        decay = jnp.exp(_dot(expo, lf_ref[:, cols].astype(BF16)))
        q = q_ref[:, cols]
        k = k_ref[:, cols]
        vb = v_ref[:, cols].astype(BF16)
        st = st_scr[hd]
        o = _dot_nt((q * decay[0:length]).astype(BF16), st.astype(BF16))
        scores = _dot_nt(q.astype(BF16), k.astype(BF16)) * mask_ref[levels]
        for m in range(levels):
            half = length >> (m + 1)
            side = jnp.where((row_idx & half) != 0, q, k)
            xm = (side * decay[(2 + m) * length:(3 + m) * length]).astype(BF16)
            scores = scores + _dot_nt(xm, xm) * mask_ref[m]
        o = o + _dot(scores.astype(BF16), vb)
        k_dec = (k * decay[length:2 * length]).astype(BF16)
        st_scr[hd] = st * decay[length - 1:length] + _dot_tn(vb, k_dec)
        o_ref[:, cols] = (_rms(o, onw_ref[...]) * g_ref[:, cols]).astype(BF16)

    @pl.when(c == pl.num_programs(1) - 1)
    def _():
        for hd in range(HG_HEADS):
            s_ref[0, hd] = st_scr[hd].T


def _hgrn(hq, hk, lf, hv, hg, s0, onw, o_alias, row0, seq):
    n, hw = hq.shape
    b = s0.shape[0]
    length = 1
    while length * 2 <= min(seq, 128) and seq % (length * 2) == 0:
        length *= 2
    levels = int(math.log2(length))
    nc = seq // length
    assert row0 % length == 0
    blk0 = row0 // length
    expo, masks = _hgrn_consts(length)
    row = lambda i, c: (blk0 + i * nc + c, 0)
    rows = pl.BlockSpec((length, hw), row)
    state = pl.BlockSpec((1, HG_HEADS, HG_DIM, HG_DIM), lambda i, c: (i, 0, 0, 0))
    kern = functools.partial(_hgrn_kernel, length=length, levels=levels)
    return pl.pallas_call(
        kern,
        out_shape=[jax.ShapeDtypeStruct((n, hw), BF16), jax.ShapeDtypeStruct(s0.shape, F32)],
        grid=(b, nc),
        in_specs=[rows, rows, rows, rows, rows, state,
                  pl.BlockSpec(expo.shape, lambda i, c: (0, 0)),
                  pl.BlockSpec(masks.shape, lambda i, c: (0, 0, 0)),
                  pl.BlockSpec(onw.shape, lambda i, c: (0, 0)),
                  pl.BlockSpec(memory_space=pl.ANY)],
        out_specs=[rows, state],
        scratch_shapes=[pltpu.VMEM((HG_HEADS, HG_DIM, HG_DIM), F32)],
        input_output_aliases={9: 0},
        compiler_params=_params("arbitrary", "arbitrary"),
    )(hq, hk, lf, hv, hg, s0, expo, masks, onw, o_alias)


def _attn_prompt_kernel(q_ref, k_ref, v_ref, alias_ref, o_ref, *, tq, nq):
    del alias_ref
    qc = lax.broadcasted_iota(jnp.int32, (tq, tq), 0) // CHUNK
    kc = lax.broadcasted_iota(jnp.int32, (tq, tq), 1) // CHUNK
    visible = kc <= qc
    for i in range(nq):
        own = slice(i * tq, (i + 1) * tq)
        q = q_ref[own, :]
        sd = jnp.where(visible, _dot_nt(q, k_ref[own, :]), -1e30)
        m = jnp.max(sd, axis=-1, keepdims=True)
        if i > 0:
            sp = _dot_nt(q, k_ref[0:i * tq, :])
            m = jnp.maximum(m, jnp.max(sp, axis=-1, keepdims=True))
        pd = jnp.exp(sd - m)
        l = jnp.sum(pd, axis=-1, keepdims=True)
        acc = _dot(pd.astype(BF16), v_ref[own, :])
        if i > 0:
            pp = jnp.exp(sp - m)
            l = l + jnp.sum(pp, axis=-1, keepdims=True)
            acc = acc + _dot(pp.astype(BF16), v_ref[0:i * tq, :])
        o_ref[own, :] = acc / l


def _attn_prompt(q_all, k_all, v_all, o_alias, b, seq):
    n = q_all.shape[0]
    tq = _pick_tile(seq, 256, CHUNK)
    nq = seq // tq
    kern = functools.partial(_attn_prompt_kernel, tq=tq, nq=nq)
    per_seq = lambda w: pl.BlockSpec((seq, w), lambda bi, h: (bi, h))
    return pl.pallas_call(
        kern,
        out_shape=jax.ShapeDtypeStruct((n, MLA_HEADS * VDIM), F32),
        grid=(b, MLA_HEADS),
        in_specs=[per_seq(QK_PAD), per_seq(QK_PAD), per_seq(VDIM), pl.BlockSpec(memory_space=pl.ANY)],
        out_specs=per_seq(VDIM),
        input_output_aliases={3: 0},
        compiler_params=_params("arbitrary", "arbitrary"),
    )(q_all, k_all, v_all, o_alias)


def _attn_sample_kernel(q_ref, kp_ref, vp_ref, kn_ref, vn_ref, alias_ref, o_ref):
    del alias_ref
    q = q_ref[...]
    sp = _dot_nt(q, kp_ref[...])
    sn = _dot_nt(q, kn_ref[...])
    m = jnp.maximum(jnp.max(sp, axis=-1, keepdims=True), jnp.max(sn, axis=-1, keepdims=True))
    pp = jnp.exp(sp - m)
    pn = jnp.exp(sn - m)
    l = jnp.sum(pp, axis=-1, keepdims=True) + jnp.sum(pn, axis=-1, keepdims=True)
    acc = _dot(pp.astype(BF16), vp_ref[...]) + _dot(pn.astype(BF16), vn_ref[...])
    o_ref[...] = acc / l


def _attn_sample(q_all, k_all, v_all, k_past, v_past, o_alias, b, seq, past, row0):
    assert row0 % seq == 0
    blk0 = row0 // seq
    new_q = lambda bi, h: (blk0 + bi, h)
    return pl.pallas_call(
        _attn_sample_kernel,
        out_shape=jax.ShapeDtypeStruct(o_alias.shape, F32),
        grid=(b, MLA_HEADS),
        in_specs=[pl.BlockSpec((seq, QK_PAD), new_q),
                  pl.BlockSpec((past, QK_PAD), lambda bi, h: (bi, h)),
                  pl.BlockSpec((past, VDIM), lambda bi, h: (bi, h)),
                  pl.BlockSpec((seq, QK_PAD), new_q),
                  pl.BlockSpec((seq, VDIM), new_q),
                  pl.BlockSpec(memory_space=pl.ANY)],
        out_specs=pl.BlockSpec((seq, VDIM), new_q),
        input_output_aliases={5: 0},
        compiler_params=_params("arbitrary", "arbitrary"),
    )(q_all, k_past, v_past, k_all, v_all, o_alias)


def _out_kernel(ohg_ref, omla_ref, x_ref, g1_ref, sh2_ref, sc2_ref, monw_ref, wo_ref, n2w_ref,
                x1_ref, h2_ref, *, groups):
    hw = ohg_ref.shape[1]
    om = _rms(omla_ref[...], monw_ref[...]).astype(BF16)
    y = _dot(ohg_ref[...], wo_ref[0:hw, :]) + _dot(om, wo_ref[hw:, :])
    n2w = n2w_ref[...]
    for g in range(groups):
        rows = slice(g * GROUP, (g + 1) * GROUP)
        x1 = x_ref[rows, :] + g1_ref[g:g + 1, :] * y[rows, :]
        x1_ref[rows, :] = x1
        h2 = _rms(x1, n2w) * (1.0 + sc2_ref[g:g + 1, :]) + sh2_ref[g:g + 1, :]
        h2_ref[rows, :] = h2.astype(h2_ref.dtype)


def _out_proj(o_hg, o_mla, x, mod_g, monw, w_o, n2w, h2_dtype):
    n, d = x.shape
    tm = _pick_tile(n, 256, GROUP)
    groups = tm // GROUP
    row = lambda i: (i, 0)
    fixed = lambda i: (0, 0)
    modspec = lambda col: pl.BlockSpec((groups, d), lambda i: (i, col))
    return pl.pallas_call(
        functools.partial(_out_kernel, groups=groups),
        out_shape=[jax.ShapeDtypeStruct((n, d), F32), jax.ShapeDtypeStruct((n, d), h2_dtype)],
        grid=(n // tm,),
        in_specs=[pl.BlockSpec((tm, o_hg.shape[1]), row), pl.BlockSpec((tm, o_mla.shape[1]), row),
                  pl.BlockSpec((tm, d), row), modspec(2), modspec(3), modspec(4),
                  pl.BlockSpec(monw.shape, fixed), pl.BlockSpec(w_o.shape, fixed),
                  pl.BlockSpec(n2w.shape, fixed)],
        out_specs=[pl.BlockSpec((tm, d), row), pl.BlockSpec((tm, d), row)],
        compiler_params=_params("arbitrary"),
    )(o_hg, o_mla, x, mod_g, mod_g, mod_g, monw, w_o, n2w)


def _ffn_kernel(h_ref, wg_ref, wu_ref, wd_ref, x1_ref, g2_ref, o_ref, acc_scr, *, groups):
    j = pl.program_id(1)

    @pl.when(j == 0)
    def _():
        acc_scr[...] = jnp.zeros_like(acc_scr)

    h = h_ref[...]
    a = (_silu(_dot(h, wg_ref[...])) * _dot(h, wu_ref[...])).astype(BF16)
    acc_scr[...] += _dot(a, wd_ref[...])

    @pl.when(j == pl.num_programs(1) - 1)
    def _():
        for g in range(groups):
            rows = slice(g * GROUP, (g + 1) * GROUP)
            o_ref[rows, :] = x1_ref[rows, :] + g2_ref[g:g + 1, :] * acc_scr[rows, :]


def _ffn(h2, wg, wu, wd, x1, mod_g):
    n, d = x1.shape
    f = wg.shape[1]
    tm = _pick_tile(n, 1280, GROUP)
    tf = _pick_tile(f, 512, LANES)
    groups = tm // GROUP
    return pl.pallas_call(
        functools.partial(_ffn_kernel, groups=groups),
        out_shape=jax.ShapeDtypeStruct((n, d), F32),
        grid=(n // tm, f // tf),
        in_specs=[pl.BlockSpec((tm, d), lambda i, j: (i, 0)),
                  pl.BlockSpec((d, tf), lambda i, j: (0, j)),
                  pl.BlockSpec((d, tf), lambda i, j: (0, j)),
                  pl.BlockSpec((tf, d), lambda i, j: (j, 0)),
                  pl.BlockSpec((tm, d), lambda i, j: (i, 0)),
                  pl.BlockSpec((groups, d), lambda i, j: (i, 5))],
        out_specs=pl.BlockSpec((tm, d), lambda i, j: (i, 0)),
        scratch_shapes=[pltpu.VMEM((tm, d), F32)],
        compiler_params=_params("arbitrary", "arbitrary"),
    )(h2, wg, wu, wd, x1, mod_g)


def _router_kernel(h_ref, rhi_ref, rlo_ref, tril_ref, pos_ref, gate_ref, cnt_ref, cnt_scr):
    i = pl.program_id(0)

    @pl.when(i == 0)
    def _():
        cnt_scr[...] = jnp.zeros_like(cnt_scr)

    h_hi, h_lo = _split_bf16(h_ref[...])
    logits = _dot(h_hi, rhi_ref[...]) + (_dot(h_hi, rlo_ref[...]) + _dot(h_lo, rhi_ref[...]))
    lane = lax.broadcasted_iota(jnp.int32, logits.shape, 1).astype(F32)
    logits = jnp.where(lane < N_EXPERTS, logits, -1e30)
    m1 = jnp.max(logits, axis=-1, keepdims=True)
    i1 = jnp.min(jnp.where(logits == m1, lane, float(LANES)), axis=-1, keepdims=True)
    hot1 = lane == i1
    rest = jnp.where(hot1, -1e30, logits)
    m2 = jnp.max(rest, axis=-1, keepdims=True)
    i2 = jnp.min(jnp.where(rest == m2, lane, float(LANES)), axis=-1, keepdims=True)
    hot2 = lane == i2
    e = jnp.exp(m2 - m1)
    gate1 = 1.0 / (1.0 + e)
    gate2 = e / (1.0 + e)
    gate_ref[...] = jnp.where(lane == 0, gate1, jnp.where(lane == 1, gate2, 0.0))
    hot = jnp.where(hot1, 1.0, jnp.where(hot2, 1.0, 0.0))
    before = _dot(tril_ref[...], hot.astype(BF16)) + cnt_scr[0:1, :]
    rank1 = jnp.sum(jnp.where(hot1, before, 0.0), axis=-1, keepdims=True)
    rank2 = jnp.sum(jnp.where(hot2, before, 0.0), axis=-1, keepdims=True)
    info = jnp.where(lane == 0, rank1, jnp.where(lane == 1, rank2,
                     jnp.where(lane == 2, i1, jnp.where(lane == 3, i2, 0.0))))
    pos_ref[...] = info.astype(jnp.int32)
    total = cnt_scr[0:1, :] + jnp.sum(hot, axis=0, keepdims=True)
    cnt_scr[...] = jnp.broadcast_to(total, cnt_scr.shape)
    cnt_ref[...] = jnp.broadcast_to(total, cnt_ref.shape).astype(jnp.int32)


def _router(h2, router):
    n, d = h2.shape
    tm = _pick_tile(n, 256, GROUP)
    r_pad = jnp.zeros((d, LANES), F32).at[:, :N_EXPERTS].set(router)
    r_hi = r_pad.astype(BF16)
    r_lo = (r_pad - r_hi.astype(F32)).astype(BF16)
    tril = jnp.asarray(np.tril(np.ones((tm, tm), np.float32), -1), BF16)
    row = lambda i: (i, 0)
    fixed = lambda i: (0, 0)
    return pl.pallas_call(
        _router_kernel,
        out_shape=[jax.ShapeDtypeStruct((n, LANES), jnp.int32), jax.ShapeDtypeStruct((n, LANES), F32),
                   jax.ShapeDtypeStruct((8, LANES), jnp.int32)],
        grid=(n // tm,),
        in_specs=[pl.BlockSpec((tm, d), row), pl.BlockSpec((d, LANES), fixed),
                  pl.BlockSpec((d, LANES), fixed), pl.BlockSpec((tm, tm), fixed)],
        out_specs=[pl.BlockSpec((tm, LANES), row), pl.BlockSpec((tm, LANES), row),
                   pl.BlockSpec((8, LANES), fixed)],
        scratch_shapes=[pltpu.VMEM((8, LANES), F32)],
        compiler_params=_params("arbitrary"),
    )(h2, r_hi, r_lo, tril)


def _row_copy(src, src_row, dst, dst_row, sem):
    return pltpu.make_async_copy(src.at[pl.ds(src_row, 1), :], dst.at[pl.ds(dst_row, 1), :], sem)


def _scatter_kernel(cnt_ref, off_ref, nu_ref, pos_ref, h_ref, xs_ref, zero_scr, sem, *, tm, tile, n_tiles):
    i = pl.program_id(0)

    def start(r, _):
        _row_copy(h_ref, r, xs_ref, pos_ref[0, 0, 2 * r], sem).start()
        _row_copy(h_ref, r, xs_ref, pos_ref[0, 0, 2 * r + 1], sem).start()
        return 0

    lax.fori_loop(0, tm, start, 0, unroll=16)
    for _ in range(2 * tm):
        _row_copy(h_ref, 0, xs_ref, 0, sem).wait()

    @pl.when(i == pl.num_programs(0) - 1)
    def _():
        zero_scr[...] = jnp.zeros_like(zero_scr)
        for e in range(N_EXPERTS):
            cnt = cnt_ref[e]
            end = ((cnt + tile - 1) // tile) * tile

            def fill(r, _):
                _row_copy(zero_scr, 0, xs_ref, off_ref[e] + r, sem).start()
                return 0

            def fill_wait(r, _):
                _row_copy(zero_scr, 0, xs_ref, 0, sem).wait()
                return 0

            lax.fori_loop(cnt, end, fill, 0)
            lax.fori_loop(cnt, end, fill_wait, 0)

        def fill_tile(t, _):
            cp = pltpu.make_async_copy(zero_scr, xs_ref.at[pl.ds(pl.multiple_of(t * tile, tile), tile), :], sem)
            cp.start()
            cp.wait()
            return 0

        lax.fori_loop(nu_ref[0], n_tiles, fill_tile, 0)


def _scatter(h2, pos2, counts, offsets, n_used, tile, n_tiles):
    n, d = h2.shape
    tm = pos2.shape[2] // 2
    grid_spec = pltpu.PrefetchScalarGridSpec(
        num_scalar_prefetch=3,
        grid=(n // tm,),
        in_specs=[pl.BlockSpec((1, 1, 2 * tm), lambda i, *_: (i, 0, 0), memory_space=pltpu.SMEM),
                  pl.BlockSpec((tm, d), lambda i, *_: (i, 0))],
        out_specs=pl.BlockSpec(memory_space=pl.ANY),
        scratch_shapes=[pltpu.VMEM((tile, d), F32), pltpu.SemaphoreType.DMA(())],
    )
    return pl.pallas_call(
        functools.partial(_scatter_kernel, tm=tm, tile=tile, n_tiles=n_tiles),
        out_shape=jax.ShapeDtypeStruct((n_tiles * tile, d), F32),
        grid_spec=grid_spec,
        compiler_params=_params("arbitrary"),
    )(counts, offsets, n_used, pos2, h2)


def _moe_kernel(te_ref, nu_ref, x_ref, wg_ref, wu_ref, wd_ref, y_ref, xb_scr, acc_scr):
    i = pl.program_id(0)
    j = pl.program_id(1)

    @pl.when((i >= nu_ref[0]) & (j == 0))
    def _():
        y_ref[...] = jnp.zeros_like(y_ref)

    @pl.when(i < nu_ref[0])
    def _():
        @pl.when(j == 0)
        def _():
            xb_scr[...] = x_ref[...].astype(BF16)
            acc_scr[...] = jnp.zeros_like(acc_scr)

        x = xb_scr[...]
        a = (_silu(_dot(x, wg_ref[0])) * _dot(x, wu_ref[0])).astype(BF16)
        acc_scr[...] += _dot(a, wd_ref[0])

        @pl.when(j == pl.num_programs(1) - 1)
        def _():
            y_ref[...] = acc_scr[...]


def _moe(xs, wg, wu, wd, tile_e, n_used, tm):
    rows, d = xs.shape
    f = wg.shape[2]
    tf = _pick_tile(f, 512, LANES)
    nf = f // tf

    def fcol(i, j, nu):
        return jnp.where(i < nu[0], j, nf - 1)

    grid_spec = pltpu.PrefetchScalarGridSpec(
        num_scalar_prefetch=2,
        grid=(rows // tm, nf),
        in_specs=[pl.BlockSpec((tm, d), lambda i, j, te, nu: (i, 0)),
                  pl.BlockSpec((1, d, tf), lambda i, j, te, nu: (te[i], 0, fcol(i, j, nu))),
                  pl.BlockSpec((1, d, tf), lambda i, j, te, nu: (te[i], 0, fcol(i, j, nu))),
                  pl.BlockSpec((1, tf, d), lambda i, j, te, nu: (te[i], fcol(i, j, nu), 0))],
        out_specs=pl.BlockSpec((tm, d), lambda i, j, te, nu: (i, 0)),
        scratch_shapes=[pltpu.VMEM((tm, d), BF16), pltpu.VMEM((tm, d), F32)],
    )
    return pl.pallas_call(
        _moe_kernel,
        out_shape=jax.ShapeDtypeStruct((rows, d), F32),
        grid_spec=grid_spec,
        compiler_params=_params("arbitrary", "arbitrary"),
    )(tile_e, n_used, xs, wg, wu, wd)


def _combine_kernel(pos_ref, gate_ref, x1_ref, g2_ref, y_ref, o_ref, buf_scr, sem, *, tm, groups):
    def start(r, _):
        pltpu.make_async_copy(y_ref.at[pl.ds(pos_ref[0, 0, 2 * r], 1), :],
                              buf_scr.at[0, pl.ds(r, 1), :], sem).start()
        pltpu.make_async_copy(y_ref.at[pl.ds(pos_ref[0, 0, 2 * r + 1], 1), :],
                              buf_scr.at[1, pl.ds(r, 1), :], sem).start()
        return 0

    lax.fori_loop(0, tm, start, 0, unroll=16)
    for _ in range(2 * tm):
        pltpu.make_async_copy(y_ref.at[pl.ds(0, 1), :], buf_scr.at[0, pl.ds(0, 1), :], sem).wait()
    for g in range(groups):
        rows = slice(g * GROUP, (g + 1) * GROUP)
        f = gate_ref[rows, 0:1] * buf_scr[0, rows, :] + gate_ref[rows, 1:2] * buf_scr[1, rows, :]
        o_ref[rows, :] = x1_ref[rows, :] + g2_ref[g:g + 1, :] * f


def _combine(y, pos2, gates, x1, mod_g):
    n, d = x1.shape
    tm = pos2.shape[2] // 2
    groups = tm // GROUP
    return pl.pallas_call(
        functools.partial(_combine_kernel, tm=tm, groups=groups),
        out_shape=jax.ShapeDtypeStruct((n, d), F32),
        grid=(n // tm,),
        in_specs=[pl.BlockSpec((1, 1, 2 * tm), lambda i: (i, 0, 0), memory_space=pltpu.SMEM),
                  pl.BlockSpec((tm, LANES), lambda i: (i, 0)),
                  pl.BlockSpec((tm, d), lambda i: (i, 0)),
                  pl.BlockSpec((groups, d), lambda i: (i, 5)),
                  pl.BlockSpec(memory_space=pl.ANY)],
        out_specs=pl.BlockSpec((tm, d), lambda i: (i, 0)),
        scratch_shapes=[pltpu.VMEM((2, tm, d), F32), pltpu.SemaphoreType.DMA(())],
        compiler_params=_params("arbitrary"),
    )(pos2, gates, x1, mod_g, y)


def _routed_ffn(h2, router, wg, wu, wd, x1, mod_g):
    n, d = x1.shape
    tile = _pick_tile(n, 640, GROUP)
    n_tiles = (2 * n) // tile + N_EXPERTS
    info, gates, counts = _router(h2, router)
    counts = counts[0, :N_EXPERTS]
    tiles_per = (counts + tile - 1) // tile
    ends = jnp.cumsum(tiles_per)
    offsets = ((ends - tiles_per) * tile).astype(jnp.int32)
    n_used = ends[-1:].astype(jnp.int32)
    t = jnp.minimum(jnp.arange(n_tiles, dtype=jnp.int32), n_used - 1)
    tile_e = jnp.sum(t[:, None] >= ends[None, :], axis=1).astype(jnp.int32)
    tm = _pick_tile(n, 256, GROUP)
    pos2 = (offsets[info[:, 2:4]] + info[:, 0:2]).reshape(n // tm, 1, 2 * tm)
    xs = _scatter(h2, pos2, counts, offsets, n_used, tile, n_tiles)
    y = _moe(xs, wg, wu, wd, tile_e, n_used, tile)
    return _combine(y, pos2, gates, x1, mod_g)


def _swap_halves(a):
    half = a.shape[-1] // 2
    return jnp.concatenate([a[..., half:], a[..., :half]], axis=-1)


def kernel(x_prompt, x_sample, c_prompt, c_sample, state_hgrn, cache_ckv, cache_kpe, ada_w, ada_b, norm1_w, norm2_w, w_in, hg_lower_bounds, hg_onorm_w, mla_qa_norm_w, mla_wqb, mla_kva_norm_w, mla_wkvb, mla_qn_nope_w, mla_qn_rope_w, mla_kn_nope_w, mla_kn_rope_w, mla_onorm_w, w_o, ffn_w_gate, ffn_w_up, ffn_w_down, moe_router, moe_w_gate, moe_w_up, moe_w_down):
    bp, sp, d = x_prompt.shape
    bs, ss, _ = x_sample.shape
    depth = ada_w.shape[0]
    past = cache_ckv.shape[2]
    n_p, n_s = bp * sp, bs * ss
    n = n_p + n_s
    hw = HG_HEADS * HG_DIM
    q_lora = mla_qa_norm_w.shape[1]
    kv_lora = mla_kva_norm_w.shape[1]
    assert sp % GROUP == 0 and ss % GROUP == 0 and n_p % ss == 0
    scale = float((NOPE + ROPE) ** -0.5)

    x = jnp.concatenate([x_prompt.reshape(n_p, d), x_sample.reshape(n_s, d)], axis=0)
    mod = _modulation(jnp.concatenate([c_prompt, c_sample], axis=0), ada_w, ada_b)
    mod_g = jnp.concatenate([jnp.repeat(mod[:, :bp], sp // GROUP, axis=1),
                             jnp.repeat(mod[:, bp:], ss // GROUP, axis=1)], axis=1)

    lw = jax.nn.softmax(hg_lower_bounds.astype(F32), axis=0)
    lower = jnp.clip(jnp.cumsum(lw, axis=0) - lw[0], 0.0, 1.0)

    half = ROPE // 2
    inv_freq = ROPE_BASE ** (-jnp.arange(half, dtype=F32) / half)

    def rope_rows(pos):
        ang = pos.astype(F32)[:, None] * inv_freq[None, :]
        return jnp.concatenate([jnp.cos(ang), jnp.cos(ang), -jnp.sin(ang), jnp.sin(ang)], axis=-1)

    rope = jnp.concatenate([jnp.tile(rope_rows(jnp.arange(sp)), (bp, 1)),
                            jnp.tile(rope_rows(past + jnp.arange(ss)), (bs, 1))], axis=0)

    row2 = lambda a: a.reshape(1, -1)
    pair = lambda w: row2(jnp.concatenate([w, _swap_halves(w)]))
    zeros_state = jnp.zeros((bp,) + state_hgrn.shape[2:], F32)

    new_s_p, new_s_s, new_ckv, new_kpe = [], [], [], []
    for l in range(depth):
        kpe_cols = w_in[l][:, -ROPE:]
        w_in_l = jnp.concatenate([w_in[l], _swap_halves(kpe_cols)], axis=1).astype(BF16)
        wqb3 = mla_wqb[l].reshape(q_lora, MLA_HEADS, NOPE + ROPE)
        q_rope = wqb3[:, :, NOPE:]
        wqb_l = jnp.concatenate(
            [wqb3[:, :, :NOPE].reshape(q_lora, MLA_HEADS * NOPE),
             jnp.concatenate([q_rope, _swap_halves(q_rope)], axis=-1).reshape(q_lora, MLA_HEADS * LANES)],
            axis=1).astype(BF16)
        wkvb_l = mla_wkvb[l].astype(BF16)
        knw = row2(mla_kn_nope_w[l])

        hq, hk, lf, hv, hg, ckv, kpe, q_all, k_all, v_all = _front(
            x, mod_g[l], row2(norm1_w[l]), w_in_l, row2(lower[l]), row2(mla_qa_norm_w[l]), wqb_l,
            row2(mla_qn_nope_w[l]), pair(mla_qn_rope_w[l]), row2(mla_kva_norm_w[l]), wkvb_l, knw,
            pair(mla_kn_rope_w[l]), rope, scale)

        onw = row2(hg_onorm_w[l])
        o_hg = jnp.zeros((n, hw), BF16)
        o_hg, s_p = _hgrn(hq, hk, lf, hv, hg, zeros_state, onw, o_hg, 0, sp)
        o_hg, s_s = _hgrn(hq, hk, lf, hv, hg, state_hgrn[l].astype(F32), onw, o_hg, n_p, ss)

        kpe_past = jnp.pad(cache_kpe[l].reshape(bs * past, ROPE), ((0, 0), (0, LANES - ROPE)))
        k_past, v_past = _kv_up(cache_ckv[l].reshape(bs * past, kv_lora), kpe_past, wkvb_l, knw)
        o_mla = _attn_prompt(q_all, k_all, v_all, jnp.zeros((n, MLA_HEADS * VDIM), F32), bp, sp)
        o_mla = _attn_sample(q_all, k_all, v_all, k_past, v_past, o_mla, bs, ss, past, n_p)

        moe_layer = l % 2 == 1
        x1, h2 = _out_proj(o_hg, o_mla, x, mod_g[l], row2(mla_onorm_w[l]), w_o[l].astype(BF16),
                           row2(norm2_w[l]), F32 if moe_layer else BF16)
        j = l // 2
        if moe_layer:
            x = _routed_ffn(h2, moe_router[j], moe_w_gate[j].astype(BF16), moe_w_up[j].astype(BF16),
                            moe_w_down[j].astype(BF16), x1, mod_g[l])
        else:
            x = _ffn(h2, ffn_w_gate[j].astype(BF16), ffn_w_up[j].astype(BF16),
                     ffn_w_down[j].astype(BF16), x1, mod_g[l])

        new_s_p.append(s_p)
        new_s_s.append(s_s)
        new_ckv.append(ckv)
        new_kpe.append(kpe)

    ckv_all = jnp.stack(new_ckv)
    kpe_all = jnp.stack(new_kpe)
    return (x[:n_p].reshape(bp, sp, d), x[n_p:].reshape(bs, ss, d),
            jnp.stack(new_s_p), ckv_all[:, :n_p].reshape(depth, bp, sp, kv_lora),
            kpe_all[:, :n_p].reshape(depth, bp, sp, ROPE),
            jnp.stack(new_s_s), ckv_all[:, n_p:].reshape(depth, bs, ss, kv_lora),
            kpe_all[:, n_p:].reshape(depth, bs, ss, ROPE))
```

```python
import functools
import math

import numpy as np
import jax
import jax.numpy as jnp
from jax import lax
from jax.experimental import pallas as pl
from jax.experimental.pallas import tpu as pltpu

F32 = jnp.float32
BF16 = jnp.bfloat16
EPS = 1e-6
ROPE_BASE = 10000.0
GROUP = 32
LANES = 128
HG_HEADS = 4
HG_DIM = 128
MLA_HEADS = 4
NOPE = 128
ROPE = 64
VDIM = 128
QK_PAD = 256
CHUNK = 64
N_EXPERTS = 8
VMEM_LIMIT = 56 * 1024 * 1024

_NT = (((1,), (1,)), ((), ()))
_TN = (((0,), (0,)), ((), ()))


def _dot(a, b):
    return jnp.dot(a, b, preferred_element_type=F32)


def _dot_nt(a, b):
    return lax.dot_general(a, b, _NT, preferred_element_type=F32)


def _dot_tn(a, b):
    return lax.dot_general(a, b, _TN, preferred_element_type=F32)


def _rms(x, w):
    return x * lax.rsqrt(jnp.mean(x * x, axis=-1, keepdims=True) + EPS) * w


def _silu(x):
    return x * jax.nn.sigmoid(x)


def _split_bf16(x):
    hi = x.astype(BF16)
    lo = (x - hi.astype(F32)).astype(BF16)
    return hi, lo


def _pick_tile(n, target, mult):
    best = None
    for t in range(mult, min(n, target) + 1, mult):
        if n % t == 0:
            best = t
    assert best is not None, (n, target, mult)
    return best


def _params(*sem):
    return pltpu.CompilerParams(dimension_semantics=sem, vmem_limit_bytes=VMEM_LIMIT)


def _mod_kernel(c_ref, w_ref, b_ref, o_ref):
    sc = _silu(c_ref[...]).astype(BF16)
    o_ref[0] = _dot(sc, w_ref[0].astype(BF16)) + b_ref[0]


def _modulation(c_all, ada_w, ada_b):
    depth, d, n6 = ada_w.shape
    rows = c_all.shape[0]
    tn = _pick_tile(n6, 1536, LANES)
    return pl.pallas_call(
        _mod_kernel,
        out_shape=jax.ShapeDtypeStruct((depth, rows, n6), F32),
        grid=(depth, n6 // tn),
        in_specs=[
            pl.BlockSpec((rows, d), lambda l, j: (0, 0)),
            pl.BlockSpec((1, d, tn), lambda l, j: (l, 0, j)),
            pl.BlockSpec((1, 1, tn), lambda l, j: (l, 0, j)),
        ],
        out_specs=pl.BlockSpec((1, rows, tn), lambda l, j: (l, 0, j)),
        compiler_params=_params("arbitrary", "arbitrary"),
    )(c_all, ada_w, ada_b.reshape(depth, 1, n6))


def _rope128(g, w128, table):
    t = _rms(g, w128) * table
    return t + pltpu.roll(t, ROPE, 1)


def _front_kernel(x_ref, mod_ref, n1w_ref, win_ref, lb_ref, qaw_ref, wqb_ref, qnw_ref, qrw_ref,
                  kvaw_ref, wkvb_ref, knw_ref, krw_ref, rope_ref,
                  hq_ref, hk_ref, lf_ref, hv_ref, hg_ref, ckv_ref, kpe_ref, q_ref, k_ref, v_ref,
                  h_scr, *, groups, q_lora, kv_lora, scale):
    d = x_ref.shape[1]
    hw = HG_HEADS * HG_DIM
    n1w = n1w_ref[...]
    for g in range(groups):
        rows = slice(g * GROUP, (g + 1) * GROUP)
        shift = mod_ref[g:g + 1, 0:d]
        gain = mod_ref[g:g + 1, d:2 * d]
        h_scr[rows, :] = (_rms(x_ref[rows, :], n1w) * (1.0 + gain) + shift).astype(BF16)
    h = h_scr[...]

    hq_ref[...] = _silu(_dot(h, win_ref[:, 0:hw]))
    one_minus_f = (1.0 - lb_ref[...]) * jax.nn.sigmoid(-_dot(h, win_ref[:, hw:2 * hw]))
    hk_ref[...] = one_minus_f
    lf_ref[...] = jnp.log1p(-one_minus_f)
    hv_ref[...] = _dot(h, win_ref[:, 2 * hw:3 * hw])
    hg_ref[...] = _silu(_dot(h, win_ref[:, 3 * hw:4 * hw]))

    o = 4 * hw
    table = rope_ref[...]
    cq = _rms(_dot(h, win_ref[:, o:o + q_lora]), qaw_ref[...]).astype(BF16)
    qh = _dot(cq, wqb_ref[...])
    nw = MLA_HEADS * NOPE
    for hd in range(MLA_HEADS):
        qn = _rms(qh[:, hd * NOPE:(hd + 1) * NOPE], qnw_ref[...]) * scale
        qr = _rope128(qh[:, nw + hd * LANES:nw + (hd + 1) * LANES], qrw_ref[...], table) * scale
        q_ref[:, hd * QK_PAD:hd * QK_PAD + NOPE] = qn.astype(BF16)
        q_ref[:, hd * QK_PAD + NOPE:(hd + 1) * QK_PAD] = qr.astype(BF16)

    o += q_lora
    ckv = _rms(_dot(h, win_ref[:, o:o + kv_lora]), kvaw_ref[...])
    ckv_ref[...] = ckv
    o += kv_lora
    kp = _rope128(_dot(h, win_ref[:, o:o + LANES]), krw_ref[...], table)
    kpe_ref[...] = kp[:, 0:ROPE]
    lane = lax.broadcasted_iota(jnp.int32, kp.shape, 1)
    kp_pad = jnp.where(lane < ROPE, kp, 0.0).astype(BF16)
    kv = _dot(ckv.astype(BF16), wkvb_ref[...])
    for hd in range(MLA_HEADS):
        c0 = hd * (NOPE + VDIM)
        k_ref[:, hd * QK_PAD:hd * QK_PAD + NOPE] = _rms(kv[:, c0:c0 + NOPE], knw_ref[...]).astype(BF16)
        k_ref[:, hd * QK_PAD + NOPE:(hd + 1) * QK_PAD] = kp_pad
        v_ref[:, hd * VDIM:(hd + 1) * VDIM] = kv[:, c0 + NOPE:c0 + NOPE + VDIM].astype(BF16)


def _front(x, mod_g, n1w, w_in, lb, qaw, wqb, qnw, qrw, kvaw, wkvb, knw, krw, rope, scale):
    n, d = x.shape
    tm = _pick_tile(n, 256, GROUP)
    groups = tm // GROUP
    hw = HG_HEADS * HG_DIM
    q_lora, kv_lora = qaw.shape[1], kvaw.shape[1]
    row = lambda i: (i, 0)
    fixed = lambda i: (0, 0)

    def full(a):
        return pl.BlockSpec(a.shape, fixed)

    f32_out = lambda w: jax.ShapeDtypeStruct((n, w), F32)
    out_shape = [f32_out(hw)] * 5 + [f32_out(kv_lora), f32_out(ROPE),
                                     jax.ShapeDtypeStruct((n, MLA_HEADS * QK_PAD), BF16),
                                     jax.ShapeDtypeStruct((n, MLA_HEADS * QK_PAD), BF16),
                                     jax.ShapeDtypeStruct((n, MLA_HEADS * VDIM), BF16)]
    out_specs = [pl.BlockSpec((tm, s.shape[1]), row) for s in out_shape]
    kern = functools.partial(_front_kernel, groups=groups, q_lora=q_lora, kv_lora=kv_lora, scale=scale)
    return pl.pallas_call(
        kern,
        out_shape=out_shape,
        grid=(n // tm,),
        in_specs=[
            pl.BlockSpec((tm, d), row),
            pl.BlockSpec((groups, 2 * d), row),
            full(n1w), full(w_in), full(lb), full(qaw), full(wqb), full(qnw), full(qrw),
            full(kvaw), full(wkvb), full(knw), full(krw),
            pl.BlockSpec((tm, LANES), row),
        ],
        out_specs=out_specs,
        scratch_shapes=[pltpu.VMEM((tm, d), BF16)],
        compiler_params=_params("arbitrary"),
    )(x, mod_g, n1w, w_in, lb, qaw, wqb, qnw, qrw, kvaw, wkvb, knw, krw, rope)


def _kvup_kernel(ckv_ref, kpe_ref, wkvb_ref, knw_ref, k_ref, v_ref):
    kv = _dot(ckv_ref[...].astype(BF16), wkvb_ref[...])
    kp = kpe_ref[...].astype(BF16)
    for hd in range(MLA_HEADS):
        c0 = hd * (NOPE + VDIM)
        k_ref[:, hd * QK_PAD:hd * QK_PAD + NOPE] = _rms(kv[:, c0:c0 + NOPE], knw_ref[...]).astype(BF16)
        k_ref[:, hd * QK_PAD + NOPE:(hd + 1) * QK_PAD] = kp
        v_ref[:, hd * VDIM:(hd + 1) * VDIM] = kv[:, c0 + NOPE:c0 + NOPE + VDIM].astype(BF16)


def _kv_up(ckv, kpe_pad, wkvb, knw):
    n, r = ckv.shape
    tm = _pick_tile(n, 512, 16)
    row = lambda i: (i, 0)
    fixed = lambda i: (0, 0)
    return pl.pallas_call(
        _kvup_kernel,
        out_shape=[jax.ShapeDtypeStruct((n, MLA_HEADS * QK_PAD), BF16),
                   jax.ShapeDtypeStruct((n, MLA_HEADS * VDIM), BF16)],
        grid=(n // tm,),
        in_specs=[pl.BlockSpec((tm, r), row), pl.BlockSpec((tm, LANES), row),
                  pl.BlockSpec(wkvb.shape, fixed), pl.BlockSpec(knw.shape, fixed)],
        out_specs=[pl.BlockSpec((tm, MLA_HEADS * QK_PAD), row), pl.BlockSpec((tm, MLA_HEADS * VDIM), row)],
        compiler_params=_params("arbitrary"),
    )(ckv, kpe_pad, wkvb, knw)


def _hgrn_consts(length):
    levels = int(math.log2(length))
    assert 1 << levels == length
    coarse = _coarse_levels(levels)
    fine = levels - coarse
    expo = np.zeros(((1 + fine) * length, length), np.float32)
    idx = np.arange(length)
    for t in range(length):
        expo[t, :t + 1] = 1.0
    masks = np.zeros((levels + 1, length, length), np.float32)
    xor = idx[:, None] ^ idx[None, :]
    for m in range(levels):
        block = length >> m
        half = block >> 1
        for t in range(length):
            if m < coarse:
                break
            row = (1 + m - coarse) * length + t
            mid = (t // block) * block + half - 1
            if t > mid:
                expo[row, mid + 1:t + 1] = 1.0
            else:
                expo[row, t + 1:mid + 1] = 1.0
        masks[m] = (idx[:, None] > idx[None, :]) & (xor >= half) & (xor < block)
    masks[levels] = np.eye(length)
    return jnp.asarray(expo, BF16), jnp.asarray(masks, F32)


def _coarse_levels(levels):
    return max(levels - 3, 0)


def _hgrn_kernel(q_ref, k_ref, lf_ref, v_ref, g_ref, s0_ref, expo_ref, mask_ref, onw_ref, alias_ref,
                 o_ref, s_ref, st_scr, *, length, levels, chunks):
    del alias_ref
    c = pl.program_id(1)

    @pl.when(c == 0)
    def _():
        for hd in range(HG_HEADS):
            st_scr[hd] = s0_ref[0, hd].T

    row_idx = lax.broadcasted_iota(jnp.int32, (length, HG_DIM), 0)
    expo = expo_ref[...]
    coarse = _coarse_levels(levels)
    states = [st_scr[hd] for hd in range(HG_HEADS)]
    for ci in range(chunks):
        rows = slice(ci * length, (ci + 1) * length)
        for hd in range(HG_HEADS):
            cols = slice(hd * HG_DIM, (hd + 1) * HG_DIM)
            sums = _dot(expo, lf_ref[rows, cols].astype(BF16))
            cum = sums[0:length]
            from_start = jnp.exp(cum)
            q = q_ref[rows, cols]
            k = k_ref[rows, cols]
            vb = v_ref[rows, cols].astype(BF16)
            st = states[hd]
            o = _dot_nt((q * from_start).astype(BF16), st.astype(BF16))
            scores = _dot_nt(q.astype(BF16), k.astype(BF16)) * mask_ref[levels]
            for m in range(levels):
                half = length >> (m + 1)
                block = 2 * half
                upper = (row_idx & half) != 0
                if m < coarse:
                    mid = jnp.concatenate(
                        [jnp.broadcast_to(cum[j * block + half - 1:j * block + half, :], (block, HG_DIM))
                         for j in range(length // block)], axis=0)
                    expn = jnp.where(upper, cum - mid, mid - cum)
                else:
                    expn = sums[(1 + m - coarse) * length:(2 + m - coarse) * length]
                xm = (jnp.where(upper, q, k) * jnp.exp(expn)).astype(BF16)
                scores = scores + _dot_nt(xm, xm) * mask_ref[m]
            o = o + _dot(scores.astype(BF16), vb)
            k_dec = (k * jnp.exp(cum[length - 1:length] - cum)).astype(BF16)
            states[hd] = st * from_start[length - 1:length] + _dot_tn(vb, k_dec)
            o_ref[rows, cols] = (_rms(o, onw_ref[...]) * g_ref[rows, cols]).astype(BF16)
    for hd in range(HG_HEADS):
        st_scr[hd] = states[hd]

    @pl.when(c == pl.num_programs(1) - 1)
    def _():
        for hd in range(HG_HEADS):
            s_ref[0, hd] = st_scr[hd].T


def _hgrn(hq, hk, lf, hv, hg, s0, onw, o_alias, row0, seq):
    n, hw = hq.shape
    b = s0.shape[0]
    length = 1
    while length * 2 <= min(seq, 128) and seq % (length * 2) == 0:
        length *= 2
    levels = int(math.log2(length))
    chunks = 2 if (seq // length) % 2 == 0 else 1
    step_rows = chunks * length
    nc = seq // step_rows
    assert row0 % step_rows == 0
    blk0 = row0 // step_rows
    expo, masks = _hgrn_consts(length)
    row = lambda i, c: (blk0 + i * nc + c, 0)
    rows = pl.BlockSpec((step_rows, hw), row)
    state = pl.BlockSpec((1, HG_HEADS, HG_DIM, HG_DIM), lambda i, c: (i, 0, 0, 0))
    kern = functools.partial(_hgrn_kernel, length=length, levels=levels, chunks=chunks)
    return pl.pallas_call(
        kern,
        out_shape=[jax.ShapeDtypeStruct((n, hw), BF16), jax.ShapeDtypeStruct(s0.shape, F32)],
        grid=(b, nc),
        in_specs=[rows, rows, rows, rows, rows, state,
                  pl.BlockSpec(expo.shape, lambda i, c: (0, 0)),
                  pl.BlockSpec(masks.shape, lambda i, c: (0, 0, 0)),
                  pl.BlockSpec(onw.shape, lambda i, c: (0, 0)),
                  pl.BlockSpec(memory_space=pl.ANY)],
        out_specs=[rows, state],
        scratch_shapes=[pltpu.VMEM((HG_HEADS, HG_DIM, HG_DIM), F32)],
        input_output_aliases={9: 0},
        compiler_params=_params("arbitrary", "arbitrary"),
    )(hq, hk, lf, hv, hg, s0, expo, masks, onw, o_alias)


def _attn_prompt_kernel(q_ref, k_ref, v_ref, alias_ref, o_ref, *, tq, nq):
    del alias_ref
    qc = lax.broadcasted_iota(jnp.int32, (tq, tq), 0) // CHUNK
    kc = lax.broadcasted_iota(jnp.int32, (tq, tq), 1) // CHUNK
    visible = kc <= qc
    for i in range(nq):
        own = slice(i * tq, (i + 1) * tq)
        q = q_ref[own, :]
        sd = jnp.where(visible, _dot_nt(q, k_ref[own, :]), -1e30)
        m = jnp.max(sd, axis=-1, keepdims=True)
        if i > 0:
            sp = _dot_nt(q, k_ref[0:i * tq, :])
            m = jnp.maximum(m, jnp.max(sp, axis=-1, keepdims=True))
        pd = jnp.exp(sd - m)
        l = jnp.sum(pd, axis=-1, keepdims=True)
        acc = _dot(pd.astype(BF16), v_ref[own, :])
        if i > 0:
            pp = jnp.exp(sp - m)
            l = l + jnp.sum(pp, axis=-1, keepdims=True)
            acc = acc + _dot(pp.astype(BF16), v_ref[0:i * tq, :])
        o_ref[own, :] = acc / l


def _attn_prompt(q_all, k_all, v_all, o_alias, b, seq):
    n = q_all.shape[0]
    tq = _pick_tile(seq, 256, CHUNK)
    nq = seq // tq
    kern = functools.partial(_attn_prompt_kernel, tq=tq, nq=nq)
    per_seq = lambda w: pl.BlockSpec((seq, w), lambda bi, h: (bi, h))
    return pl.pallas_call(
        kern,
        out_shape=jax.ShapeDtypeStruct((n, MLA_HEADS * VDIM), F32),
        grid=(b, MLA_HEADS),
        in_specs=[per_seq(QK_PAD), per_seq(QK_PAD), per_seq(VDIM), pl.BlockSpec(memory_space=pl.ANY)],
        out_specs=per_seq(VDIM),
        input_output_aliases={3: 0},
        compiler_params=_params("arbitrary", "arbitrary"),
    )(q_all, k_all, v_all, o_alias)


def _attn_sample_kernel(q_ref, kp_ref, vp_ref, kn_ref, vn_ref, alias_ref, o_ref):
    del alias_ref
    q = q_ref[...]
    sp = _dot_nt(q, kp_ref[...])
    sn = _dot_nt(q, kn_ref[...])
    m = jnp.maximum(jnp.max(sp, axis=-1, keepdims=True), jnp.max(sn, axis=-1, keepdims=True))
    pp = jnp.exp(sp - m)
    pn = jnp.exp(sn - m)
    l = jnp.sum(pp, axis=-1, keepdims=True) + jnp.sum(pn, axis=-1, keepdims=True)
    acc = _dot(pp.astype(BF16), vp_ref[...]) + _dot(pn.astype(BF16), vn_ref[...])
    o_ref[...] = acc / l


def _attn_sample(q_all, k_all, v_all, k_past, v_past, o_alias, b, seq, past, row0):
    assert row0 % seq == 0
    blk0 = row0 // seq
    new_q = lambda bi, h: (blk0 + bi, h)
    return pl.pallas_call(
        _attn_sample_kernel,
        out_shape=jax.ShapeDtypeStruct(o_alias.shape, F32),
        grid=(b, MLA_HEADS),
        in_specs=[pl.BlockSpec((seq, QK_PAD), new_q),
                  pl.BlockSpec((past, QK_PAD), lambda bi, h: (bi, h)),
                  pl.BlockSpec((past, VDIM), lambda bi, h: (bi, h)),
                  pl.BlockSpec((seq, QK_PAD), new_q),
                  pl.BlockSpec((seq, VDIM), new_q),
                  pl.BlockSpec(memory_space=pl.ANY)],
        out_specs=pl.BlockSpec((seq, VDIM), new_q),
        input_output_aliases={5: 0},
        compiler_params=_params("arbitrary", "arbitrary"),
    )(q_all, k_past, v_past, k_all, v_all, o_alias)


def _out_kernel(ohg_ref, omla_ref, x_ref, g1_ref, sh2_ref, sc2_ref, monw_ref, wo_ref, n2w_ref,
                x1_ref, h2_ref, *, groups):
    hw = ohg_ref.shape[1]
    om = _rms(omla_ref[...], monw_ref[...]).astype(BF16)
    y = _dot(ohg_ref[...], wo_ref[0:hw, :]) + _dot(om, wo_ref[hw:, :])
    n2w = n2w_ref[...]
    for g in range(groups):
        rows = slice(g * GROUP, (g + 1) * GROUP)
        x1 = x_ref[rows, :] + g1_ref[g:g + 1, :] * y[rows, :]
        x1_ref[rows, :] = x1
        h2 = _rms(x1, n2w) * (1.0 + sc2_ref[g:g + 1, :]) + sh2_ref[g:g + 1, :]
        h2_ref[rows, :] = h2.astype(h2_ref.dtype)


def _out_proj(o_hg, o_mla, x, mod_g, monw, w_o, n2w, h2_dtype):
    n, d = x.shape
    tm = _pick_tile(n, 256, GROUP)
    groups = tm // GROUP
    row = lambda i: (i, 0)
    fixed = lambda i: (0, 0)
    modspec = lambda col: pl.BlockSpec((groups, d), lambda i: (i, col))
    return pl.pallas_call(
        functools.partial(_out_kernel, groups=groups),
        out_shape=[jax.ShapeDtypeStruct((n, d), F32), jax.ShapeDtypeStruct((n, d), h2_dtype)],
        grid=(n // tm,),
        in_specs=[pl.BlockSpec((tm, o_hg.shape[1]), row), pl.BlockSpec((tm, o_mla.shape[1]), row),
                  pl.BlockSpec((tm, d), row), modspec(2), modspec(3), modspec(4),
                  pl.BlockSpec(monw.shape, fixed), pl.BlockSpec(w_o.shape, fixed),
                  pl.BlockSpec(n2w.shape, fixed)],
        out_specs=[pl.BlockSpec((tm, d), row), pl.BlockSpec((tm, d), row)],
        compiler_params=_params("arbitrary"),
    )(o_hg, o_mla, x, mod_g, mod_g, mod_g, monw, w_o, n2w)


def _ffn_kernel(h_ref, wg_ref, wu_ref, wd_ref, x1_ref, g2_ref, o_ref, acc_scr, *, groups):
    j = pl.program_id(1)

    @pl.when(j == 0)
    def _():
        acc_scr[...] = jnp.zeros_like(acc_scr)

    h = h_ref[...]
    a = (_silu(_dot(h, wg_ref[...])) * _dot(h, wu_ref[...])).astype(BF16)
    acc_scr[...] += _dot(a, wd_ref[...])

    @pl.when(j == pl.num_programs(1) - 1)
    def _():
        for g in range(groups):
            rows = slice(g * GROUP, (g + 1) * GROUP)
            o_ref[rows, :] = x1_ref[rows, :] + g2_ref[g:g + 1, :] * acc_scr[rows, :]


def _ffn(h2, wg, wu, wd, x1, mod_g):
    n, d = x1.shape
    f = wg.shape[1]
    tm = _pick_tile(n, 1280, GROUP)
    tf = _pick_tile(f, 512, LANES)
    groups = tm // GROUP
    return pl.pallas_call(
        functools.partial(_ffn_kernel, groups=groups),
        out_shape=jax.ShapeDtypeStruct((n, d), F32),
        grid=(n // tm, f // tf),
        in_specs=[pl.BlockSpec((tm, d), lambda i, j: (i, 0)),
                  pl.BlockSpec((d, tf), lambda i, j: (0, j)),
                  pl.BlockSpec((d, tf), lambda i, j: (0, j)),
                  pl.BlockSpec((tf, d), lambda i, j: (j, 0)),
                  pl.BlockSpec((tm, d), lambda i, j: (i, 0)),
                  pl.BlockSpec((groups, d), lambda i, j: (i, 5))],
        out_specs=pl.BlockSpec((tm, d), lambda i, j: (i, 0)),
        scratch_shapes=[pltpu.VMEM((tm, d), F32)],
        compiler_params=_params("arbitrary", "arbitrary"),
    )(h2, wg, wu, wd, x1, mod_g)


def _router_kernel(h_ref, rhi_ref, rlo_ref, tril_ref, pos_ref, gate_ref, cnt_ref, cnt_scr):
    i = pl.program_id(0)

    @pl.when(i == 0)
    def _():
        cnt_scr[...] = jnp.zeros_like(cnt_scr)

    h_hi, h_lo = _split_bf16(h_ref[...])
    logits = _dot(h_hi, rhi_ref[...]) + (_dot(h_hi, rlo_ref[...]) + _dot(h_lo, rhi_ref[...]))
    lane = lax.broadcasted_iota(jnp.int32, logits.shape, 1).astype(F32)
    logits = jnp.where(lane < N_EXPERTS, logits, -1e30)
    m1 = jnp.max(logits, axis=-1, keepdims=True)
    i1 = jnp.min(jnp.where(logits == m1, lane, float(LANES)), axis=-1, keepdims=True)
    hot1 = lane == i1
    rest = jnp.where(hot1, -1e30, logits)
    m2 = jnp.max(rest, axis=-1, keepdims=True)
    i2 = jnp.min(jnp.where(rest == m2, lane, float(LANES)), axis=-1, keepdims=True)
    hot2 = lane == i2
    e = jnp.exp(m2 - m1)
    gate1 = 1.0 / (1.0 + e)
    gate2 = e / (1.0 + e)
    gate_ref[...] = jnp.where(lane == 0, gate1, jnp.where(lane == 1, gate2, 0.0))
    hot = jnp.where(hot1, 1.0, jnp.where(hot2, 1.0, 0.0))
    before = _dot(tril_ref[...], hot.astype(BF16)) + cnt_scr[0:1, :]
    rank1 = jnp.sum(jnp.where(hot1, before, 0.0), axis=-1, keepdims=True)
    rank2 = jnp.sum(jnp.where(hot2, before, 0.0), axis=-1, keepdims=True)
    info = jnp.where(lane == 0, rank1, jnp.where(lane == 1, rank2,
                     jnp.where(lane == 2, i1, jnp.where(lane == 3, i2, 0.0))))
    pos_ref[...] = info.astype(jnp.int32)
    total = cnt_scr[0:1, :] + jnp.sum(hot, axis=0, keepdims=True)
    cnt_scr[...] = jnp.broadcast_to(total, cnt_scr.shape)
    cnt_ref[...] = jnp.broadcast_to(total, cnt_ref.shape).astype(jnp.int32)


def _router(h2, router):
    n, d = h2.shape
    tm = _pick_tile(n, 1280, GROUP)
    r_pad = jnp.zeros((d, LANES), F32).at[:, :N_EXPERTS].set(router)
    r_hi = r_pad.astype(BF16)
    r_lo = (r_pad - r_hi.astype(F32)).astype(BF16)
    tril = jnp.asarray(np.tril(np.ones((tm, tm), np.float32), -1), BF16)
    row = lambda i: (i, 0)
    fixed = lambda i: (0, 0)
    return pl.pallas_call(
        _router_kernel,
        out_shape=[jax.ShapeDtypeStruct((n, LANES), jnp.int32), jax.ShapeDtypeStruct((n, LANES), F32),
                   jax.ShapeDtypeStruct((8, LANES), jnp.int32)],
        grid=(n // tm,),
        in_specs=[pl.BlockSpec((tm, d), row), pl.BlockSpec((d, LANES), fixed),
                  pl.BlockSpec((d, LANES), fixed), pl.BlockSpec((tm, tm), fixed)],
        out_specs=[pl.BlockSpec((tm, LANES), row), pl.BlockSpec((tm, LANES), row),
                   pl.BlockSpec((8, LANES), fixed)],
        scratch_shapes=[pltpu.VMEM((8, LANES), F32)],
        compiler_params=_params("arbitrary"),
    )(h2, r_hi, r_lo, tril)


def _row_copy(src, src_row, dst, dst_row, sem):
    return pltpu.make_async_copy(src.at[pl.ds(src_row, 1), :], dst.at[pl.ds(dst_row, 1), :], sem)


def _scatter_kernel(cnt_ref, off_ref, nu_ref, pos_ref, h_ref, xs_ref, zero_scr, sem, *, tm, tile, n_tiles):
    i = pl.program_id(0)

    def start(r, _):
        _row_copy(h_ref, r, xs_ref, pos_ref[0, 0, 2 * r], sem).start()
        _row_copy(h_ref, r, xs_ref, pos_ref[0, 0, 2 * r + 1], sem).start()
        return 0

    lax.fori_loop(0, tm, start, 0, unroll=16)
    for _ in range(2 * tm):
        _row_copy(h_ref, 0, xs_ref, 0, sem).wait()

    @pl.when(i == pl.num_programs(0) - 1)
    def _():
        zero_scr[...] = jnp.zeros_like(zero_scr)
        for e in range(N_EXPERTS):
            cnt = cnt_ref[e]
            end = ((cnt + tile - 1) // tile) * tile

            def fill(r, _):
                _row_copy(zero_scr, 0, xs_ref, off_ref[e] + r, sem).start()
                return 0

            def fill_wait(r, _):
                _row_copy(zero_scr, 0, xs_ref, 0, sem).wait()
                return 0

            lax.fori_loop(cnt, end, fill, 0)
            lax.fori_loop(cnt, end, fill_wait, 0)

        def fill_tile(t, _):
            cp = pltpu.make_async_copy(zero_scr, xs_ref.at[pl.ds(pl.multiple_of(t * tile, tile), tile), :], sem)
            cp.start()
            cp.wait()
            return 0

        lax.fori_loop(nu_ref[0], n_tiles, fill_tile, 0)


def _scatter(h2, pos2, counts, offsets, n_used, tile, n_tiles):
    n, d = h2.shape
    tm = pos2.shape[2] // 2
    grid_spec = pltpu.PrefetchScalarGridSpec(
        num_scalar_prefetch=3,
        grid=(n // tm,),
        in_specs=[pl.BlockSpec((1, 1, 2 * tm), lambda i, *_: (i, 0, 0), memory_space=pltpu.SMEM),
                  pl.BlockSpec((tm, d), lambda i, *_: (i, 0))],
        out_specs=pl.BlockSpec(memory_space=pl.ANY),
        scratch_shapes=[pltpu.VMEM((tile, d), F32), pltpu.SemaphoreType.DMA(())],
    )
    return pl.pallas_call(
        functools.partial(_scatter_kernel, tm=tm, tile=tile, n_tiles=n_tiles),
        out_shape=jax.ShapeDtypeStruct((n_tiles * tile, d), F32),
        grid_spec=grid_spec,
        compiler_params=_params("arbitrary"),
    )(counts, offsets, n_used, pos2, h2)


def _moe_kernel(te_ref, nu_ref, x_ref, wg_ref, wu_ref, wd_ref, y_ref, xb_scr, acc_scr):
    i = pl.program_id(0)
    j = pl.program_id(1)

    @pl.when((i >= nu_ref[0]) & (j == 0))
    def _():
        y_ref[...] = jnp.zeros_like(y_ref)

    @pl.when(i < nu_ref[0])
    def _():
        @pl.when(j == 0)
        def _():
            xb_scr[...] = x_ref[...].astype(BF16)
            acc_scr[...] = jnp.zeros_like(acc_scr)

        x = xb_scr[...]
        a = (_silu(_dot(x, wg_ref[0])) * _dot(x, wu_ref[0])).astype(BF16)
        acc_scr[...] += _dot(a, wd_ref[0])

        @pl.when(j == pl.num_programs(1) - 1)
        def _():
            y_ref[...] = acc_scr[...]


def _moe(xs, wg, wu, wd, tile_e, n_used, tm):
    rows, d = xs.shape
    f = wg.shape[2]
    tf = _pick_tile(f, 512, LANES)
    nf = f // tf

    def fcol(i, j, nu):
        return jnp.where(i < nu[0], j, nf - 1)

    grid_spec = pltpu.PrefetchScalarGridSpec(
        num_scalar_prefetch=2,
        grid=(rows // tm, nf),
        in_specs=[pl.BlockSpec((tm, d), lambda i, j, te, nu: (i, 0)),
                  pl.BlockSpec((1, d, tf), lambda i, j, te, nu: (te[i], 0, fcol(i, j, nu))),
                  pl.BlockSpec((1, d, tf), lambda i, j, te, nu: (te[i], 0, fcol(i, j, nu))),
                  pl.BlockSpec((1, tf, d), lambda i, j, te, nu: (te[i], fcol(i, j, nu), 0))],
        out_specs=pl.BlockSpec((tm, d), lambda i, j, te, nu: (i, 0)),
        scratch_shapes=[pltpu.VMEM((tm, d), BF16), pltpu.VMEM((tm, d), F32)],
    )
    return pl.pallas_call(
        _moe_kernel,
        out_shape=jax.ShapeDtypeStruct((rows, d), F32),
        grid_spec=grid_spec,
        compiler_params=_params("arbitrary", "arbitrary"),
    )(tile_e, n_used, xs, wg, wu, wd)


def _combine_kernel(pos_ref, gate_ref, x1_ref, g2_ref, y_ref, o_ref, buf_scr, sem, *, tm, groups):
    def start(r, _):
        pltpu.make_async_copy(y_ref.at[pl.ds(pos_ref[0, 0, 2 * r], 1), :],
                              buf_scr.at[0, pl.ds(r, 1), :], sem).start()
        pltpu.make_async_copy(y_ref.at[pl.ds(pos_ref[0, 0, 2 * r + 1], 1), :],
                              buf_scr.at[1, pl.ds(r, 1), :], sem).start()
        return 0

    lax.fori_loop(0, tm, start, 0, unroll=16)
    for _ in range(2 * tm):
        pltpu.make_async_copy(y_ref.at[pl.ds(0, 1), :], buf_scr.at[0, pl.ds(0, 1), :], sem).wait()
    for g in range(groups):
        rows = slice(g * GROUP, (g + 1) * GROUP)
        f = gate_ref[rows, 0:1] * buf_scr[0, rows, :] + gate_ref[rows, 1:2] * buf_scr[1, rows, :]
        o_ref[rows, :] = x1_ref[rows, :] + g2_ref[g:g + 1, :] * f


def _combine(y, pos2, gates, x1, mod_g):
    n, d = x1.shape
    tm = pos2.shape[2] // 2
    groups = tm // GROUP
    return pl.pallas_call(
        functools.partial(_combine_kernel, tm=tm, groups=groups),
        out_shape=jax.ShapeDtypeStruct((n, d), F32),
        grid=(n // tm,),
        in_specs=[pl.BlockSpec((1, 1, 2 * tm), lambda i: (i, 0, 0), memory_space=pltpu.SMEM),
                  pl.BlockSpec((tm, LANES), lambda i: (i, 0)),
                  pl.BlockSpec((tm, d), lambda i: (i, 0)),
                  pl.BlockSpec((groups, d), lambda i: (i, 5)),
                  pl.BlockSpec(memory_space=pl.ANY)],
        out_specs=pl.BlockSpec((tm, d), lambda i: (i, 0)),
        scratch_shapes=[pltpu.VMEM((2, tm, d), F32), pltpu.SemaphoreType.DMA(())],
        compiler_params=_params("arbitrary"),
    )(pos2, gates, x1, mod_g, y)


def _routed_ffn(h2, router, wg, wu, wd, x1, mod_g):
    n, d = x1.shape
    tile = _pick_tile(n, 640, GROUP)
    n_tiles = (2 * n) // tile + N_EXPERTS
    info, gates, counts = _router(h2, router)
    counts = counts[0, :N_EXPERTS]
    tiles_per = (counts + tile - 1) // tile
    ends = jnp.cumsum(tiles_per)
    offsets = ((ends - tiles_per) * tile).astype(jnp.int32)
    n_used = ends[-1:].astype(jnp.int32)
    t = jnp.minimum(jnp.arange(n_tiles, dtype=jnp.int32), n_used - 1)
    tile_e = jnp.sum(t[:, None] >= ends[None, :], axis=1).astype(jnp.int32)
    tm = _pick_tile(n, 256, GROUP)
    pos2 = (offsets[info[:, 2:4]] + info[:, 0:2]).reshape(n // tm, 1, 2 * tm)
    xs = _scatter(h2, pos2, counts, offsets, n_used, tile, n_tiles)
    y = _moe(xs, wg, wu, wd, tile_e, n_used, tile)
    return _combine(y, pos2, gates, x1, mod_g)


def _swap_halves(a):
    half = a.shape[-1] // 2
    return jnp.concatenate([a[..., half:], a[..., :half]], axis=-1)


def kernel(x_prompt, x_sample, c_prompt, c_sample, state_hgrn, cache_ckv, cache_kpe, ada_w, ada_b, norm1_w, norm2_w, w_in, hg_lower_bounds, hg_onorm_w, mla_qa_norm_w, mla_wqb, mla_kva_norm_w, mla_wkvb, mla_qn_nope_w, mla_qn_rope_w, mla_kn_nope_w, mla_kn_rope_w, mla_onorm_w, w_o, ffn_w_gate, ffn_w_up, ffn_w_down, moe_router, moe_w_gate, moe_w_up, moe_w_down):
    bp, sp, d = x_prompt.shape
    bs, ss, _ = x_sample.shape
    depth = ada_w.shape[0]
    past = cache_ckv.shape[2]
    n_p, n_s = bp * sp, bs * ss
    n = n_p + n_s
    hw = HG_HEADS * HG_DIM
    q_lora = mla_qa_norm_w.shape[1]
    kv_lora = mla_kva_norm_w.shape[1]
    assert sp % GROUP == 0 and ss % GROUP == 0 and n_p % ss == 0
    scale = float((NOPE + ROPE) ** -0.5)

    x = jnp.concatenate([x_prompt.reshape(n_p, d), x_sample.reshape(n_s, d)], axis=0)
    mod = _modulation(jnp.concatenate([c_prompt, c_sample], axis=0), ada_w, ada_b)
    mod_g = jnp.concatenate([jnp.repeat(mod[:, :bp], sp // GROUP, axis=1),
                             jnp.repeat(mod[:, bp:], ss // GROUP, axis=1)], axis=1)

    lw = jax.nn.softmax(hg_lower_bounds.astype(F32), axis=0)
    lower = jnp.clip(jnp.cumsum(lw, axis=0) - lw[0], 0.0, 1.0)

    half = ROPE // 2
    inv_freq = ROPE_BASE ** (-jnp.arange(half, dtype=F32) / half)

    def rope_rows(pos):
        ang = pos.astype(F32)[:, None] * inv_freq[None, :]
        return jnp.concatenate([jnp.cos(ang), jnp.cos(ang), -jnp.sin(ang), jnp.sin(ang)], axis=-1)

    rope = jnp.concatenate([jnp.tile(rope_rows(jnp.arange(sp)), (bp, 1)),
                            jnp.tile(rope_rows(past + jnp.arange(ss)), (bs, 1))], axis=0)

    row2 = lambda a: a.reshape(1, -1)
    pair = lambda w: row2(jnp.concatenate([w, _swap_halves(w)]))
    zeros_state = jnp.zeros((bp,) + state_hgrn.shape[2:], F32)

    new_s_p, new_s_s, new_ckv, new_kpe = [], [], [], []
    for l in range(depth):
        kpe_cols = w_in[l][:, -ROPE:]
        w_in_l = jnp.concatenate([w_in[l], _swap_halves(kpe_cols)], axis=1).astype(BF16)
        wqb3 = mla_wqb[l].reshape(q_lora, MLA_HEADS, NOPE + ROPE)
        q_rope = wqb3[:, :, NOPE:]
        wqb_l = jnp.concatenate(
            [wqb3[:, :, :NOPE].reshape(q_lora, MLA_HEADS * NOPE),
             jnp.concatenate([q_rope, _swap_halves(q_rope)], axis=-1).reshape(q_lora, MLA_HEADS * LANES)],
            axis=1).astype(BF16)
        wkvb_l = mla_wkvb[l].astype(BF16)
        knw = row2(mla_kn_nope_w[l])

        hq, hk, lf, hv, hg, ckv, kpe, q_all, k_all, v_all = _front(
            x, mod_g[l], row2(norm1_w[l]), w_in_l, row2(lower[l]), row2(mla_qa_norm_w[l]), wqb_l,
            row2(mla_qn_nope_w[l]), pair(mla_qn_rope_w[l]), row2(mla_kva_norm_w[l]), wkvb_l, knw,
            pair(mla_kn_rope_w[l]), rope, scale)

        onw = row2(hg_onorm_w[l])
        o_hg = jnp.zeros((n, hw), BF16)
        o_hg, s_p = _hgrn(hq, hk, lf, hv, hg, zeros_state, onw, o_hg, 0, sp)
        o_hg, s_s = _hgrn(hq, hk, lf, hv, hg, state_hgrn[l].astype(F32), onw, o_hg, n_p, ss)

        kpe_past = jnp.pad(cache_kpe[l].reshape(bs * past, ROPE), ((0, 0), (0, LANES - ROPE)))
        k_past, v_past = _kv_up(cache_ckv[l].reshape(bs * past, kv_lora), kpe_past, wkvb_l, knw)
        o_mla = _attn_prompt(q_all, k_all, v_all, jnp.zeros((n, MLA_HEADS * VDIM), F32), bp, sp)
        o_mla = _attn_sample(q_all, k_all, v_all, k_past, v_past, o_mla, bs, ss, past, n_p)

        moe_layer = l % 2 == 1
        x1, h2 = _out_proj(o_hg, o_mla, x, mod_g[l], row2(mla_onorm_w[l]), w_o[l].astype(BF16),
                           row2(norm2_w[l]), F32 if moe_layer else BF16)
        j = l // 2
        if moe_layer:
            x = _routed_ffn(h2, moe_router[j], moe_w_gate[j].astype(BF16), moe_w_up[j].astype(BF16),
                            moe_w_down[j].astype(BF16), x1, mod_g[l])
        else:
            x = _ffn(h2, ffn_w_gate[j].astype(BF16), ffn_w_up[j].astype(BF16),
                     ffn_w_down[j].astype(BF16), x1, mod_g[l])

        new_s_p.append(s_p)
        new_s_s.append(s_s)
        new_ckv.append(ckv)
        new_kpe.append(kpe)

    ckv_all = jnp.stack(new_ckv)
    kpe_all = jnp.stack(new_kpe)
    return (x[:n_p].reshape(bp, sp, d), x[n_p:].reshape(bs, ss, d),
            jnp.stack(new_s_p), ckv_all[:, :n_p].reshape(depth, bp, sp, kv_lora),
            kpe_all[:, :n_p].reshape(depth, bp, sp, ROPE),
            jnp.stack(new_s_s), ckv_all[:, n_p:].reshape(depth, bs, ss, kv_lora),
            kpe_all[:, n_p:].reshape(depth, bs, ss, ROPE))
```

```python
import functools
import math

import numpy as np
import jax
import jax.numpy as jnp
from jax import lax
from jax.experimental import pallas as pl
from jax.experimental.pallas import tpu as pltpu

F32 = jnp.float32
BF16 = jnp.bfloat16
EPS = 1e-6
ROPE_BASE = 10000.0
GROUP = 32
LANES = 128
HG_HEADS = 4
HG_DIM = 128
MLA_HEADS = 4
NOPE = 128
ROPE = 64
VDIM = 128
QK_PAD = 256
CHUNK = 64
N_EXPERTS = 8
VMEM_LIMIT = 56 * 1024 * 1024

_NT = (((1,), (1,)), ((), ()))
_TN = (((0,), (0,)), ((), ()))


def _dot(a, b):
    return jnp.dot(a, b, preferred_element_type=F32)


def _dot_nt(a, b):
    return lax.dot_general(a, b, _NT, preferred_element_type=F32)


def _dot_tn(a, b):
    return lax.dot_general(a, b, _TN, preferred_element_type=F32)


def _rms(x, w):
    return x * lax.rsqrt(jnp.mean(x * x, axis=-1, keepdims=True) + EPS) * w


def _silu(x):
    return x * jax.nn.sigmoid(x)


def _split_bf16(x):
    hi = x.astype(BF16)
    lo = (x - hi.astype(F32)).astype(BF16)
    return hi, lo


def _pick_tile(n, target, mult):
    best = None
    for t in range(mult, min(n, target) + 1, mult):
        if n % t == 0:
            best = t
    assert best is not None, (n, target, mult)
    return best


def _params(*sem):
    return pltpu.CompilerParams(dimension_semantics=sem, vmem_limit_bytes=VMEM_LIMIT)


def _mod_kernel(c_ref, w_ref, b_ref, o_ref):
    sc = _silu(c_ref[...]).astype(BF16)
    o_ref[0] = _dot(sc, w_ref[0].astype(BF16)) + b_ref[0]


def _modulation(c_all, ada_w, ada_b):
    depth, d, n6 = ada_w.shape
    rows = c_all.shape[0]
    tn = _pick_tile(n6, 1536, LANES)
    return pl.pallas_call(
        _mod_kernel,
        out_shape=jax.ShapeDtypeStruct((depth, rows, n6), F32),
        grid=(depth, n6 // tn),
        in_specs=[
            pl.BlockSpec((rows, d), lambda l, j: (0, 0)),
            pl.BlockSpec((1, d, tn), lambda l, j: (l, 0, j)),
            pl.BlockSpec((1, 1, tn), lambda l, j: (l, 0, j)),
        ],
        out_specs=pl.BlockSpec((1, rows, tn), lambda l, j: (l, 0, j)),
        compiler_params=_params("arbitrary", "arbitrary"),
    )(c_all, ada_w, ada_b.reshape(depth, 1, n6))


def _rope128(g, w128, table):
    t = _rms(g, w128) * table
    return t + pltpu.roll(t, ROPE, 1)


def _front_kernel(x_ref, mod_ref, n1w_ref, win_ref, lb_ref, qaw_ref, wqb_ref, qnw_ref, qrw_ref,
                  kvaw_ref, wkvb_ref, knw_ref, krw_ref, rope_ref,
                  hq_ref, hk_ref, lf_ref, hv_ref, hg_ref, ckv_ref, kpe_ref, q_ref, k_ref, v_ref,
                  h_scr, *, groups, q_lora, kv_lora, scale):
    d = x_ref.shape[1]
    hw = HG_HEADS * HG_DIM
    n1w = n1w_ref[...]
    for g in range(groups):
        rows = slice(g * GROUP, (g + 1) * GROUP)
        shift = mod_ref[g:g + 1, 0:d]
        gain = mod_ref[g:g + 1, d:2 * d]
        h_scr[rows, :] = (_rms(x_ref[rows, :], n1w) * (1.0 + gain) + shift).astype(BF16)
    h = h_scr[...]

    hq_ref[...] = _silu(_dot(h, win_ref[:, 0:hw]))
    one_minus_f = (1.0 - lb_ref[...]) * jax.nn.sigmoid(-_dot(h, win_ref[:, hw:2 * hw]))
    hk_ref[...] = one_minus_f
    lf_ref[...] = jnp.log1p(-one_minus_f)
    hv_ref[...] = _dot(h, win_ref[:, 2 * hw:3 * hw])
    hg_ref[...] = _silu(_dot(h, win_ref[:, 3 * hw:4 * hw]))

    o = 4 * hw
    table = rope_ref[...]
    cq = _rms(_dot(h, win_ref[:, o:o + q_lora]), qaw_ref[...]).astype(BF16)
    qh = _dot(cq, wqb_ref[...])
    nw = MLA_HEADS * NOPE
    for hd in range(MLA_HEADS):
        qn = _rms(qh[:, hd * NOPE:(hd + 1) * NOPE], qnw_ref[...]) * scale
        qr = _rope128(qh[:, nw + hd * LANES:nw + (hd + 1) * LANES], qrw_ref[...], table) * scale
        q_ref[:, hd * QK_PAD:hd * QK_PAD + NOPE] = qn.astype(BF16)
        q_ref[:, hd * QK_PAD + NOPE:(hd + 1) * QK_PAD] = qr.astype(BF16)

    o += q_lora
    ckv = _rms(_dot(h, win_ref[:, o:o + kv_lora]), kvaw_ref[...])
    ckv_ref[...] = ckv
    o += kv_lora
    kp = _rope128(_dot(h, win_ref[:, o:o + LANES]), krw_ref[...], table)
    kpe_ref[...] = kp[:, 0:ROPE]
    lane = lax.broadcasted_iota(jnp.int32, kp.shape, 1)
    kp_pad = jnp.where(lane < ROPE, kp, 0.0).astype(BF16)
    kv = _dot(ckv.astype(BF16), wkvb_ref[...])
    for hd in range(MLA_HEADS):
        c0 = hd * (NOPE + VDIM)
        k_ref[:, hd * QK_PAD:hd * QK_PAD + NOPE] = _rms(kv[:, c0:c0 + NOPE], knw_ref[...]).astype(BF16)
        k_ref[:, hd * QK_PAD + NOPE:(hd + 1) * QK_PAD] = kp_pad
        v_ref[:, hd * VDIM:(hd + 1) * VDIM] = kv[:, c0 + NOPE:c0 + NOPE + VDIM].astype(BF16)


def _front(x, mod_g, n1w, w_in, lb, qaw, wqb, qnw, qrw, kvaw, wkvb, knw, krw, rope, scale):
    n, d = x.shape
    tm = _pick_tile(n, 256, GROUP)
    groups = tm // GROUP
    hw = HG_HEADS * HG_DIM
    q_lora, kv_lora = qaw.shape[1], kvaw.shape[1]
    row = lambda i: (i, 0)
    fixed = lambda i: (0, 0)

    def full(a):
        return pl.BlockSpec(a.shape, fixed)

    f32_out = lambda w: jax.ShapeDtypeStruct((n, w), F32)
    out_shape = [f32_out(hw)] * 5 + [f32_out(kv_lora), f32_out(ROPE),
                                     jax.ShapeDtypeStruct((n, MLA_HEADS * QK_PAD), BF16),
                                     jax.ShapeDtypeStruct((n, MLA_HEADS * QK_PAD), BF16),
                                     jax.ShapeDtypeStruct((n, MLA_HEADS * VDIM), BF16)]
    out_specs = [pl.BlockSpec((tm, s.shape[1]), row) for s in out_shape]
    kern = functools.partial(_front_kernel, groups=groups, q_lora=q_lora, kv_lora=kv_lora, scale=scale)
    return pl.pallas_call(
        kern,
        out_shape=out_shape,
        grid=(n // tm,),
        in_specs=[
            pl.BlockSpec((tm, d), row),
            pl.BlockSpec((groups, 2 * d), row),
            full(n1w), full(w_in), full(lb), full(qaw), full(wqb), full(qnw), full(qrw),
            full(kvaw), full(wkvb), full(knw), full(krw),
            pl.BlockSpec((tm, LANES), row),
        ],
        out_specs=out_specs,
        scratch_shapes=[pltpu.VMEM((tm, d), BF16)],
        compiler_params=_params("arbitrary"),
    )(x, mod_g, n1w, w_in, lb, qaw, wqb, qnw, qrw, kvaw, wkvb, knw, krw, rope)


def _kvup_kernel(ckv_ref, kpe_ref, wkvb_ref, knw_ref, k_ref, v_ref):
    kv = _dot(ckv_ref[...].astype(BF16), wkvb_ref[...])
    kp = kpe_ref[...].astype(BF16)
    for hd in range(MLA_HEADS):
        c0 = hd * (NOPE + VDIM)
        k_ref[:, hd * QK_PAD:hd * QK_PAD + NOPE] = _rms(kv[:, c0:c0 + NOPE], knw_ref[...]).astype(BF16)
        k_ref[:, hd * QK_PAD + NOPE:(hd + 1) * QK_PAD] = kp
        v_ref[:, hd * VDIM:(hd + 1) * VDIM] = kv[:, c0 + NOPE:c0 + NOPE + VDIM].astype(BF16)


def _kv_up(ckv, kpe_pad, wkvb, knw):
    n, r = ckv.shape
    tm = _pick_tile(n, 512, 16)
    row = lambda i: (i, 0)
    fixed = lambda i: (0, 0)
    return pl.pallas_call(
        _kvup_kernel,
        out_shape=[jax.ShapeDtypeStruct((n, MLA_HEADS * QK_PAD), BF16),
                   jax.ShapeDtypeStruct((n, MLA_HEADS * VDIM), BF16)],
        grid=(n // tm,),
        in_specs=[pl.BlockSpec((tm, r), row), pl.BlockSpec((tm, LANES), row),
                  pl.BlockSpec(wkvb.shape, fixed), pl.BlockSpec(knw.shape, fixed)],
        out_specs=[pl.BlockSpec((tm, MLA_HEADS * QK_PAD), row), pl.BlockSpec((tm, MLA_HEADS * VDIM), row)],
        compiler_params=_params("arbitrary"),
    )(ckv, kpe_pad, wkvb, knw)


def _hgrn_consts(length):
    levels = int(math.log2(length))
    assert 1 << levels == length
    coarse = _coarse_levels(levels)
    fine = levels - coarse
    expo = np.zeros(((1 + fine) * length, length), np.float32)
    idx = np.arange(length)
    for t in range(length):
        expo[t, :t + 1] = 1.0
    masks = np.zeros((levels + 1, length, length), np.float32)
    xor = idx[:, None] ^ idx[None, :]
    for m in range(levels):
        block = length >> m
        half = block >> 1
        for t in range(length):
            if m < coarse:
                break
            row = (1 + m - coarse) * length + t
            mid = (t // block) * block + half - 1
            if t > mid:
                expo[row, mid + 1:t + 1] = 1.0
            else:
                expo[row, t + 1:mid + 1] = 1.0
        masks[m] = (idx[:, None] > idx[None, :]) & (xor >= half) & (xor < block)
    masks[levels] = np.eye(length)
    return jnp.asarray(expo, BF16), jnp.asarray(masks, F32)


def _coarse_levels(levels):
    return max(levels - 3, 0)


def _hgrn_kernel(q_ref, k_ref, lf_ref, v_ref, g_ref, s0_ref, expo_ref, mask_ref, onw_ref, alias_ref,
                 o_ref, s_ref, st_scr, *, length, levels, chunks):
    del alias_ref
    c = pl.program_id(1)

    @pl.when(c == 0)
    def _():
        for hd in range(HG_HEADS):
            st_scr[hd] = s0_ref[0, hd].T

    row_idx = lax.broadcasted_iota(jnp.int32, (length, HG_DIM), 0)
    expo = expo_ref[...]
    coarse = _coarse_levels(levels)
    states = [st_scr[hd] for hd in range(HG_HEADS)]
    for ci in range(chunks):
        rows = slice(ci * length, (ci + 1) * length)
        for hd in range(HG_HEADS):
            cols = slice(hd * HG_DIM, (hd + 1) * HG_DIM)
            sums = _dot(expo, lf_ref[rows, cols].astype(BF16))
            cum = sums[0:length]
            from_start = jnp.exp(cum)
            q = q_ref[rows, cols]
            k = k_ref[rows, cols]
            vb = v_ref[rows, cols].astype(BF16)
            st = states[hd]
            o = _dot_nt((q * from_start).astype(BF16), st.astype(BF16))
            scores = _dot_nt(q.astype(BF16), k.astype(BF16)) * mask_ref[levels]
            for m in range(levels):
                half = length >> (m + 1)
                block = 2 * half
                upper = (row_idx & half) != 0
                if m < coarse:
                    mid = jnp.concatenate(
                        [jnp.broadcast_to(cum[j * block + half - 1:j * block + half, :], (block, HG_DIM))
                         for j in range(length // block)], axis=0)
                    expn = jnp.where(upper, cum - mid, mid - cum)
                else:
                    expn = sums[(1 + m - coarse) * length:(2 + m - coarse) * length]
                xm = (jnp.where(upper, q, k) * jnp.exp(expn)).astype(BF16)
                scores = scores + _dot_nt(xm, xm) * mask_ref[m]
            o = o + _dot(scores.astype(BF16), vb)
            k_dec = (k * jnp.exp(cum[length - 1:length] - cum)).astype(BF16)
            states[hd] = st * from_start[length - 1:length] + _dot_tn(vb, k_dec)
            o_ref[rows, cols] = (_rms(o, onw_ref[...]) * g_ref[rows, cols]).astype(BF16)
    for hd in range(HG_HEADS):
        st_scr[hd] = states[hd]

    @pl.when(c == pl.num_programs(1) - 1)
    def _():
        for hd in range(HG_HEADS):
            s_ref[0, hd] = st_scr[hd].T


def _hgrn(hq, hk, lf, hv, hg, s0, onw, o_alias, row0, seq):
    n, hw = hq.shape
    b = s0.shape[0]
    length = 1
    while length * 2 <= min(seq, 128) and seq % (length * 2) == 0:
        length *= 2
    levels = int(math.log2(length))
    chunks = max(c for c in (1, 2, 4) if (seq // length) % c == 0)
    step_rows = chunks * length
    nc = seq // step_rows
    assert row0 % step_rows == 0
    blk0 = row0 // step_rows
    expo, masks = _hgrn_consts(length)
    row = lambda i, c: (blk0 + i * nc + c, 0)
    rows = pl.BlockSpec((step_rows, hw), row)
    state = pl.BlockSpec((1, HG_HEADS, HG_DIM, HG_DIM), lambda i, c: (i, 0, 0, 0))
    kern = functools.partial(_hgrn_kernel, length=length, levels=levels, chunks=chunks)
    return pl.pallas_call(
        kern,
        out_shape=[jax.ShapeDtypeStruct((n, hw), BF16), jax.ShapeDtypeStruct(s0.shape, F32)],
        grid=(b, nc),
        in_specs=[rows, rows, rows, rows, rows, state,
                  pl.BlockSpec(expo.shape, lambda i, c: (0, 0)),
                  pl.BlockSpec(masks.shape, lambda i, c: (0, 0, 0)),
                  pl.BlockSpec(onw.shape, lambda i, c: (0, 0)),
                  pl.BlockSpec(memory_space=pl.ANY)],
        out_specs=[rows, state],
        scratch_shapes=[pltpu.VMEM((HG_HEADS, HG_DIM, HG_DIM), F32)],
        input_output_aliases={9: 0},
        compiler_params=_params("arbitrary", "arbitrary"),
    )(hq, hk, lf, hv, hg, s0, expo, masks, onw, o_alias)


def _attn_prompt_kernel(q_ref, k_ref, v_ref, alias_ref, o_ref, *, tq, nq):
    del alias_ref
    qc = lax.broadcasted_iota(jnp.int32, (tq, tq), 0) // CHUNK
    kc = lax.broadcasted_iota(jnp.int32, (tq, tq), 1) // CHUNK
    visible = kc <= qc
    for i in range(nq):
        own = slice(i * tq, (i + 1) * tq)
        q = q_ref[own, :]
        sd = jnp.where(visible, _dot_nt(q, k_ref[own, :]), -1e30)
        m = jnp.max(sd, axis=-1, keepdims=True)
        if i > 0:
            sp = _dot_nt(q, k_ref[0:i * tq, :])
            m = jnp.maximum(m, jnp.max(sp, axis=-1, keepdims=True))
        pd = jnp.exp(sd - m)
        l = jnp.sum(pd, axis=-1, keepdims=True)
        acc = _dot(pd.astype(BF16), v_ref[own, :])
        if i > 0:
            pp = jnp.exp(sp - m)
            l = l + jnp.sum(pp, axis=-1, keepdims=True)
            acc = acc + _dot(pp.astype(BF16), v_ref[0:i * tq, :])
        o_ref[own, :] = acc / l


def _attn_prompt(q_all, k_all, v_all, o_alias, b, seq):
    n = q_all.shape[0]
    tq = _pick_tile(seq, 256, CHUNK)
    nq = seq // tq
    kern = functools.partial(_attn_prompt_kernel, tq=tq, nq=nq)
    per_seq = lambda w: pl.BlockSpec((seq, w), lambda bi, h: (bi, h))
    return pl.pallas_call(
        kern,
        out_shape=jax.ShapeDtypeStruct((n, MLA_HEADS * VDIM), F32),
        grid=(b, MLA_HEADS),
        in_specs=[per_seq(QK_PAD), per_seq(QK_PAD), per_seq(VDIM), pl.BlockSpec(memory_space=pl.ANY)],
        out_specs=per_seq(VDIM),
        input_output_aliases={3: 0},
        compiler_params=_params("arbitrary", "arbitrary"),
    )(q_all, k_all, v_all, o_alias)


def _attn_sample_kernel(q_ref, kp_ref, vp_ref, kn_ref, vn_ref, alias_ref, o_ref):
    del alias_ref
    q = q_ref[...]
    sp = _dot_nt(q, kp_ref[...])
    sn = _dot_nt(q, kn_ref[...])
    m = jnp.maximum(jnp.max(sp, axis=-1, keepdims=True), jnp.max(sn, axis=-1, keepdims=True))
    pp = jnp.exp(sp - m)
    pn = jnp.exp(sn - m)
    l = jnp.sum(pp, axis=-1, keepdims=True) + jnp.sum(pn, axis=-1, keepdims=True)
    acc = _dot(pp.astype(BF16), vp_ref[...]) + _dot(pn.astype(BF16), vn_ref[...])
    o_ref[...] = acc / l


def _attn_sample(q_all, k_all, v_all, k_past, v_past, o_alias, b, seq, past, row0):
    assert row0 % seq == 0
    blk0 = row0 // seq
    new_q = lambda bi, h: (blk0 + bi, h)
    return pl.pallas_call(
        _attn_sample_kernel,
        out_shape=jax.ShapeDtypeStruct(o_alias.shape, F32),
        grid=(b, MLA_HEADS),
        in_specs=[pl.BlockSpec((seq, QK_PAD), new_q),
                  pl.BlockSpec((past, QK_PAD), lambda bi, h: (bi, h)),
                  pl.BlockSpec((past, VDIM), lambda bi, h: (bi, h)),
                  pl.BlockSpec((seq, QK_PAD), new_q),
                  pl.BlockSpec((seq, VDIM), new_q),
                  pl.BlockSpec(memory_space=pl.ANY)],
        out_specs=pl.BlockSpec((seq, VDIM), new_q),
        input_output_aliases={5: 0},
        compiler_params=_params("arbitrary", "arbitrary"),
    )(q_all, k_past, v_past, k_all, v_all, o_alias)


def _out_kernel(ohg_ref, omla_ref, x_ref, g1_ref, sh2_ref, sc2_ref, monw_ref, wo_ref, n2w_ref,
                x1_ref, h2_ref, *, groups):
    hw = ohg_ref.shape[1]
    om = _rms(omla_ref[...], monw_ref[...]).astype(BF16)
    y = _dot(ohg_ref[...], wo_ref[0:hw, :]) + _dot(om, wo_ref[hw:, :])
    n2w = n2w_ref[...]
    for g in range(groups):
        rows = slice(g * GROUP, (g + 1) * GROUP)
        x1 = x_ref[rows, :] + g1_ref[g:g + 1, :] * y[rows, :]
        x1_ref[rows, :] = x1
        h2 = _rms(x1, n2w) * (1.0 + sc2_ref[g:g + 1, :]) + sh2_ref[g:g + 1, :]
        h2_ref[rows, :] = h2.astype(h2_ref.dtype)


def _out_proj(o_hg, o_mla, x, mod_g, monw, w_o, n2w, h2_dtype):
    n, d = x.shape
    tm = _pick_tile(n, 256, GROUP)
    groups = tm // GROUP
    row = lambda i: (i, 0)
    fixed = lambda i: (0, 0)
    modspec = lambda col: pl.BlockSpec((groups, d), lambda i: (i, col))
    return pl.pallas_call(
        functools.partial(_out_kernel, groups=groups),
        out_shape=[jax.ShapeDtypeStruct((n, d), F32), jax.ShapeDtypeStruct((n, d), h2_dtype)],
        grid=(n // tm,),
        in_specs=[pl.BlockSpec((tm, o_hg.shape[1]), row), pl.BlockSpec((tm, o_mla.shape[1]), row),
                  pl.BlockSpec((tm, d), row), modspec(2), modspec(3), modspec(4),
                  pl.BlockSpec(monw.shape, fixed), pl.BlockSpec(w_o.shape, fixed),
                  pl.BlockSpec(n2w.shape, fixed)],
        out_specs=[pl.BlockSpec((tm, d), row), pl.BlockSpec((tm, d), row)],
        compiler_params=_params("arbitrary"),
    )(o_hg, o_mla, x, mod_g, mod_g, mod_g, monw, w_o, n2w)


def _ffn_kernel(h_ref, wg_ref, wu_ref, wd_ref, x1_ref, g2_ref, o_ref, acc_scr, *, groups):
    j = pl.program_id(1)

    @pl.when(j == 0)
    def _():
        acc_scr[...] = jnp.zeros_like(acc_scr)

    h = h_ref[...]
    a = (_silu(_dot(h, wg_ref[...].astype(BF16))) * _dot(h, wu_ref[...].astype(BF16))).astype(BF16)
    acc_scr[...] += _dot(a, wd_ref[...].astype(BF16))

    @pl.when(j == pl.num_programs(1) - 1)
    def _():
        for g in range(groups):
            rows = slice(g * GROUP, (g + 1) * GROUP)
            o_ref[rows, :] = x1_ref[rows, :] + g2_ref[g:g + 1, :] * acc_scr[rows, :]


def _ffn(h2, wg, wu, wd, x1, mod_g):
    n, d = x1.shape
    f = wg.shape[1]
    tm = _pick_tile(n, 1280, GROUP)
    tf = _pick_tile(f, 512, LANES)
    groups = tm // GROUP
    return pl.pallas_call(
        functools.partial(_ffn_kernel, groups=groups),
        out_shape=jax.ShapeDtypeStruct((n, d), F32),
        grid=(n // tm, f // tf),
        in_specs=[pl.BlockSpec((tm, d), lambda i, j: (i, 0)),
                  pl.BlockSpec((d, tf), lambda i, j: (0, j)),
                  pl.BlockSpec((d, tf), lambda i, j: (0, j)),
                  pl.BlockSpec((tf, d), lambda i, j: (j, 0)),
                  pl.BlockSpec((tm, d), lambda i, j: (i, 0)),
                  pl.BlockSpec((groups, d), lambda i, j: (i, 5))],
        out_specs=pl.BlockSpec((tm, d), lambda i, j: (i, 0)),
        scratch_shapes=[pltpu.VMEM((tm, d), F32)],
        compiler_params=_params("arbitrary", "arbitrary"),
    )(h2, wg, wu, wd, x1, mod_g)


def _router_kernel(h_ref, rhi_ref, rlo_ref, tril_ref, pos_ref, gate_ref, cnt_ref, cnt_scr):
    i = pl.program_id(0)

    @pl.when(i == 0)
    def _():
        cnt_scr[...] = jnp.zeros_like(cnt_scr)

    h_hi, h_lo = _split_bf16(h_ref[...])
    logits = _dot(h_hi, rhi_ref[...]) + (_dot(h_hi, rlo_ref[...]) + _dot(h_lo, rhi_ref[...]))
    lane = lax.broadcasted_iota(jnp.int32, logits.shape, 1).astype(F32)
    logits = jnp.where(lane < N_EXPERTS, logits, -1e30)
    m1 = jnp.max(logits, axis=-1, keepdims=True)
    i1 = jnp.min(jnp.where(logits == m1, lane, float(LANES)), axis=-1, keepdims=True)
    hot1 = lane == i1
    rest = jnp.where(hot1, -1e30, logits)
    m2 = jnp.max(rest, axis=-1, keepdims=True)
    i2 = jnp.min(jnp.where(rest == m2, lane, float(LANES)), axis=-1, keepdims=True)
    hot2 = lane == i2
    e = jnp.exp(m2 - m1)
    gate1 = 1.0 / (1.0 + e)
    gate2 = e / (1.0 + e)
    gate_ref[...] = jnp.where(lane == 0, gate1, jnp.where(lane == 1, gate2, 0.0))
    hot = jnp.where(hot1, 1.0, jnp.where(hot2, 1.0, 0.0))
    before = _dot(tril_ref[...], hot.astype(BF16)) + cnt_scr[0:1, :]
    rank1 = jnp.sum(jnp.where(hot1, before, 0.0), axis=-1, keepdims=True)
    rank2 = jnp.sum(jnp.where(hot2, before, 0.0), axis=-1, keepdims=True)
    info = jnp.where(lane == 0, rank1, jnp.where(lane == 1, rank2,
                     jnp.where(lane == 2, i1, jnp.where(lane == 3, i2, 0.0))))
    pos_ref[...] = info.astype(jnp.int32)
    total = cnt_scr[0:1, :] + jnp.sum(hot, axis=0, keepdims=True)
    cnt_scr[...] = jnp.broadcast_to(total, cnt_scr.shape)
    cnt_ref[...] = jnp.broadcast_to(total, cnt_ref.shape).astype(jnp.int32)


def _router(h2, router):
    n, d = h2.shape
    tm = _pick_tile(n, 1280, GROUP)
    r_pad = jnp.zeros((d, LANES), F32).at[:, :N_EXPERTS].set(router)
    r_hi = r_pad.astype(BF16)
    r_lo = (r_pad - r_hi.astype(F32)).astype(BF16)
    tril = jnp.asarray(np.tril(np.ones((tm, tm), np.float32), -1), BF16)
    row = lambda i: (i, 0)
    fixed = lambda i: (0, 0)
    return pl.pallas_call(
        _router_kernel,
        out_shape=[jax.ShapeDtypeStruct((n, LANES), jnp.int32), jax.ShapeDtypeStruct((n, LANES), F32),
                   jax.ShapeDtypeStruct((8, LANES), jnp.int32)],
        grid=(n // tm,),
        in_specs=[pl.BlockSpec((tm, d), row), pl.BlockSpec((d, LANES), fixed),
                  pl.BlockSpec((d, LANES), fixed), pl.BlockSpec((tm, tm), fixed)],
        out_specs=[pl.BlockSpec((tm, LANES), row), pl.BlockSpec((tm, LANES), row),
                   pl.BlockSpec((8, LANES), fixed)],
        scratch_shapes=[pltpu.VMEM((8, LANES), F32)],
        compiler_params=_params("arbitrary"),
    )(h2, r_hi, r_lo, tril)


def _row_copy(src, src_row, dst, dst_row, sem):
    return pltpu.make_async_copy(src.at[pl.ds(src_row, 1), :], dst.at[pl.ds(dst_row, 1), :], sem)


def _scatter_kernel(cnt_ref, off_ref, nu_ref, pos_ref, h_ref, xs_ref, zero_scr, sem, *, tm, tile, n_tiles):
    i = pl.program_id(0)

    def start(r, _):
        _row_copy(h_ref, r, xs_ref, pos_ref[0, 0, 2 * r], sem).start()
        _row_copy(h_ref, r, xs_ref, pos_ref[0, 0, 2 * r + 1], sem).start()
        return 0

    lax.fori_loop(0, tm, start, 0, unroll=16)
    for _ in range(2 * tm):
        _row_copy(h_ref, 0, xs_ref, 0, sem).wait()

    @pl.when(i == pl.num_programs(0) - 1)
    def _():
        zero_scr[...] = jnp.zeros_like(zero_scr)
        for e in range(N_EXPERTS):
            cnt = cnt_ref[e]
            end = ((cnt + tile - 1) // tile) * tile

            def fill(r, _):
                _row_copy(zero_scr, 0, xs_ref, off_ref[e] + r, sem).start()
                return 0

            def fill_wait(r, _):
                _row_copy(zero_scr, 0, xs_ref, 0, sem).wait()
                return 0

            lax.fori_loop(cnt, end, fill, 0)
            lax.fori_loop(cnt, end, fill_wait, 0)

        def fill_tile(t, _):
            cp = pltpu.make_async_copy(zero_scr, xs_ref.at[pl.ds(pl.multiple_of(t * tile, tile), tile), :], sem)
            cp.start()
            cp.wait()
            return 0

        lax.fori_loop(nu_ref[0], n_tiles, fill_tile, 0)


def _scatter(h2, pos2, counts, offsets, n_used, tile, n_tiles):
    n, d = h2.shape
    tm = pos2.shape[2] // 2
    grid_spec = pltpu.PrefetchScalarGridSpec(
        num_scalar_prefetch=3,
        grid=(n // tm,),
        in_specs=[pl.BlockSpec((1, 1, 2 * tm), lambda i, *_: (i, 0, 0), memory_space=pltpu.SMEM),
                  pl.BlockSpec((tm, d), lambda i, *_: (i, 0))],
        out_specs=pl.BlockSpec(memory_space=pl.ANY),
        scratch_shapes=[pltpu.VMEM((tile, d), F32), pltpu.SemaphoreType.DMA(())],
    )
    return pl.pallas_call(
        functools.partial(_scatter_kernel, tm=tm, tile=tile, n_tiles=n_tiles),
        out_shape=jax.ShapeDtypeStruct((n_tiles * tile, d), F32),
        grid_spec=grid_spec,
        compiler_params=_params("arbitrary"),
    )(counts, offsets, n_used, pos2, h2)


def _moe_kernel(te_ref, nu_ref, x_ref, wg_ref, wu_ref, wd_ref, y_ref, xb_scr, acc_scr):
    i = pl.program_id(0)
    j = pl.program_id(1)

    @pl.when((i >= nu_ref[0]) & (j == 0))
    def _():
        y_ref[...] = jnp.zeros_like(y_ref)

    @pl.when(i < nu_ref[0])
    def _():
        @pl.when(j == 0)
        def _():
            xb_scr[...] = x_ref[...].astype(BF16)
            acc_scr[...] = jnp.zeros_like(acc_scr)

        x = xb_scr[...]
        a = (_silu(_dot(x, wg_ref[0].astype(BF16))) * _dot(x, wu_ref[0].astype(BF16))).astype(BF16)
        acc_scr[...] += _dot(a, wd_ref[0].astype(BF16))

        @pl.when(j == pl.num_programs(1) - 1)
        def _():
            y_ref[...] = acc_scr[...]


def _moe(xs, wg, wu, wd, tile_e, n_used, tm):
    rows, d = xs.shape
    f = wg.shape[2]
    tf = _pick_tile(f, 512, LANES)
    nf = f // tf

    def fcol(i, j, nu):
        return jnp.where(i < nu[0], j, nf - 1)

    grid_spec = pltpu.PrefetchScalarGridSpec(
        num_scalar_prefetch=2,
        grid=(rows // tm, nf),
        in_specs=[pl.BlockSpec((tm, d), lambda i, j, te, nu: (i, 0)),
                  pl.BlockSpec((1, d, tf), lambda i, j, te, nu: (te[i], 0, fcol(i, j, nu))),
                  pl.BlockSpec((1, d, tf), lambda i, j, te, nu: (te[i], 0, fcol(i, j, nu))),
                  pl.BlockSpec((1, tf, d), lambda i, j, te, nu: (te[i], fcol(i, j, nu), 0))],
        out_specs=pl.BlockSpec((tm, d), lambda i, j, te, nu: (i, 0)),
        scratch_shapes=[pltpu.VMEM((tm, d), BF16), pltpu.VMEM((tm, d), F32)],
    )
    return pl.pallas_call(
        _moe_kernel,
        out_shape=jax.ShapeDtypeStruct((rows, d), F32),
        grid_spec=grid_spec,
        compiler_params=_params("arbitrary", "arbitrary"),
    )(tile_e, n_used, xs, wg, wu, wd)


def _combine_kernel(pos_ref, gate_ref, x1_ref, g2_ref, y_ref, o_ref, buf_scr, sem, *, tm, groups):
    def start(r, _):
        pltpu.make_async_copy(y_ref.at[pl.ds(pos_ref[0, 0, 2 * r], 1), :],
                              buf_scr.at[0, pl.ds(r, 1), :], sem).start()
        pltpu.make_async_copy(y_ref.at[pl.ds(pos_ref[0, 0, 2 * r + 1], 1), :],
                              buf_scr.at[1, pl.ds(r, 1), :], sem).start()
        return 0

    lax.fori_loop(0, tm, start, 0, unroll=16)
    for _ in range(2 * tm):
        pltpu.make_async_copy(y_ref.at[pl.ds(0, 1), :], buf_scr.at[0, pl.ds(0, 1), :], sem).wait()
    for g in range(groups):
        rows = slice(g * GROUP, (g + 1) * GROUP)
        f = gate_ref[rows, 0:1] * buf_scr[0, rows, :] + gate_ref[rows, 1:2] * buf_scr[1, rows, :]
        o_ref[rows, :] = x1_ref[rows, :] + g2_ref[g:g + 1, :] * f


def _combine(y, pos2, gates, x1, mod_g):
    n, d = x1.shape
    tm = pos2.shape[2] // 2
    groups = tm // GROUP
    return pl.pallas_call(
        functools.partial(_combine_kernel, tm=tm, groups=groups),
        out_shape=jax.ShapeDtypeStruct((n, d), F32),
        grid=(n // tm,),
        in_specs=[pl.BlockSpec((1, 1, 2 * tm), lambda i: (i, 0, 0), memory_space=pltpu.SMEM),
                  pl.BlockSpec((tm, LANES), lambda i: (i, 0)),
                  pl.BlockSpec((tm, d), lambda i: (i, 0)),
                  pl.BlockSpec((groups, d), lambda i: (i, 5)),
                  pl.BlockSpec(memory_space=pl.ANY)],
        out_specs=pl.BlockSpec((tm, d), lambda i: (i, 0)),
        scratch_shapes=[pltpu.VMEM((2, tm, d), F32), pltpu.SemaphoreType.DMA(())],
        compiler_params=_params("arbitrary"),
    )(pos2, gates, x1, mod_g, y)


def _routed_ffn(h2, router, wg, wu, wd, x1, mod_g):
    n, d = x1.shape
    tile = _pick_tile(n, 640, GROUP)
    n_tiles = (2 * n) // tile + N_EXPERTS
    info, gates, counts = _router(h2, router)
    counts = counts[0, :N_EXPERTS]
    tiles_per = (counts + tile - 1) // tile
    ends = jnp.cumsum(tiles_per)
    offsets = ((ends - tiles_per) * tile).astype(jnp.int32)
    n_used = ends[-1:].astype(jnp.int32)
    t = jnp.minimum(jnp.arange(n_tiles, dtype=jnp.int32), n_used - 1)
    tile_e = jnp.sum(t[:, None] >= ends[None, :], axis=1).astype(jnp.int32)
    tm = _pick_tile(n, 256, GROUP)
    pos2 = (offsets[info[:, 2:4]] + info[:, 0:2]).reshape(n // tm, 1, 2 * tm)
    xs = _scatter(h2, pos2, counts, offsets, n_used, tile, n_tiles)
    y = _moe(xs, wg, wu, wd, tile_e, n_used, tile)
    return _combine(y, pos2, gates, x1, mod_g)


def _swap_halves(a):
    half = a.shape[-1] // 2
    return jnp.concatenate([a[..., half:], a[..., :half]], axis=-1)


def kernel(x_prompt, x_sample, c_prompt, c_sample, state_hgrn, cache_ckv, cache_kpe, ada_w, ada_b, norm1_w, norm2_w, w_in, hg_lower_bounds, hg_onorm_w, mla_qa_norm_w, mla_wqb, mla_kva_norm_w, mla_wkvb, mla_qn_nope_w, mla_qn_rope_w, mla_kn_nope_w, mla_kn_rope_w, mla_onorm_w, w_o, ffn_w_gate, ffn_w_up, ffn_w_down, moe_router, moe_w_gate, moe_w_up, moe_w_down):
    bp, sp, d = x_prompt.shape
    bs, ss, _ = x_sample.shape
    depth = ada_w.shape[0]
    past = cache_ckv.shape[2]
    n_p, n_s = bp * sp, bs * ss
    n = n_p + n_s
    hw = HG_HEADS * HG_DIM
    q_lora = mla_qa_norm_w.shape[1]
    kv_lora = mla_kva_norm_w.shape[1]
    assert sp % GROUP == 0 and ss % GROUP == 0 and n_p % ss == 0
    scale = float((NOPE + ROPE) ** -0.5)

    x = jnp.concatenate([x_prompt.reshape(n_p, d), x_sample.reshape(n_s, d)], axis=0)
    mod = _modulation(jnp.concatenate([c_prompt, c_sample], axis=0), ada_w, ada_b)
    mod_g = jnp.concatenate([jnp.repeat(mod[:, :bp], sp // GROUP, axis=1),
                             jnp.repeat(mod[:, bp:], ss // GROUP, axis=1)], axis=1)

    lw = jax.nn.softmax(hg_lower_bounds.astype(F32), axis=0)
    lower = jnp.clip(jnp.cumsum(lw, axis=0) - lw[0], 0.0, 1.0)

    half = ROPE // 2
    inv_freq = ROPE_BASE ** (-jnp.arange(half, dtype=F32) / half)

    def rope_rows(pos):
        ang = pos.astype(F32)[:, None] * inv_freq[None, :]
        return jnp.concatenate([jnp.cos(ang), jnp.cos(ang), -jnp.sin(ang), jnp.sin(ang)], axis=-1)

    rope = jnp.concatenate([jnp.tile(rope_rows(jnp.arange(sp)), (bp, 1)),
                            jnp.tile(rope_rows(past + jnp.arange(ss)), (bs, 1))], axis=0)

    row2 = lambda a: a.reshape(1, -1)
    pair = lambda w: row2(jnp.concatenate([w, _swap_halves(w)]))
    zeros_state = jnp.zeros((bp,) + state_hgrn.shape[2:], F32)

    new_s_p, new_s_s, new_ckv, new_kpe = [], [], [], []
    for l in range(depth):
        kpe_cols = w_in[l][:, -ROPE:]
        w_in_l = jnp.concatenate([w_in[l], _swap_halves(kpe_cols)], axis=1).astype(BF16)
        wqb3 = mla_wqb[l].reshape(q_lora, MLA_HEADS, NOPE + ROPE)
        q_rope = wqb3[:, :, NOPE:]
        wqb_l = jnp.concatenate(
            [wqb3[:, :, :NOPE].reshape(q_lora, MLA_HEADS * NOPE),
             jnp.concatenate([q_rope, _swap_halves(q_rope)], axis=-1).reshape(q_lora, MLA_HEADS * LANES)],
            axis=1).astype(BF16)
        wkvb_l = mla_wkvb[l].astype(BF16)
        knw = row2(mla_kn_nope_w[l])

        hq, hk, lf, hv, hg, ckv, kpe, q_all, k_all, v_all = _front(
            x, mod_g[l], row2(norm1_w[l]), w_in_l, row2(lower[l]), row2(mla_qa_norm_w[l]), wqb_l,
            row2(mla_qn_nope_w[l]), pair(mla_qn_rope_w[l]), row2(mla_kva_norm_w[l]), wkvb_l, knw,
            pair(mla_kn_rope_w[l]), rope, scale)

        onw = row2(hg_onorm_w[l])
        o_hg = jnp.zeros((n, hw), BF16)
        o_hg, s_p = _hgrn(hq, hk, lf, hv, hg, zeros_state, onw, o_hg, 0, sp)
        o_hg, s_s = _hgrn(hq, hk, lf, hv, hg, state_hgrn[l].astype(F32), onw, o_hg, n_p, ss)

        kpe_past = jnp.pad(cache_kpe[l].reshape(bs * past, ROPE), ((0, 0), (0, LANES - ROPE)))
        k_past, v_past = _kv_up(cache_ckv[l].reshape(bs * past, kv_lora), kpe_past, wkvb_l, knw)
        o_mla = _attn_prompt(q_all, k_all, v_all, jnp.zeros((n, MLA_HEADS * VDIM), F32), bp, sp)
        o_mla = _attn_sample(q_all, k_all, v_all, k_past, v_past, o_mla, bs, ss, past, n_p)

        moe_layer = l % 2 == 1
        x1, h2 = _out_proj(o_hg, o_mla, x, mod_g[l], row2(mla_onorm_w[l]), w_o[l].astype(BF16),
                           row2(norm2_w[l]), F32 if moe_layer else BF16)
        j = l // 2
        if moe_layer:
            x = _routed_ffn(h2, moe_router[j], moe_w_gate[j], moe_w_up[j], moe_w_down[j], x1, mod_g[l])
        else:
            x = _ffn(h2, ffn_w_gate[j], ffn_w_up[j], ffn_w_down[j], x1, mod_g[l])

        new_s_p.append(s_p)
        new_s_s.append(s_s)
        new_ckv.append(ckv)
        new_kpe.append(kpe)

    ckv_all = jnp.stack(new_ckv)
    kpe_all = jnp.stack(new_kpe)
    return (x[:n_p].reshape(bp, sp, d), x[n_p:].reshape(bs, ss, d),
            jnp.stack(new_s_p), ckv_all[:, :n_p].reshape(depth, bp, sp, kv_lora),
            kpe_all[:, :n_p].reshape(depth, bp, sp, ROPE),
            jnp.stack(new_s_s), ckv_all[:, n_p:].reshape(depth, bs, ss, kv_lora),
            kpe_all[:, n_p:].reshape(depth, bs, ss, ROPE))
```

```python
import functools
import math

import numpy as np
import jax
import jax.numpy as jnp
from jax import lax
from jax.experimental import pallas as pl
from jax.experimental.pallas import tpu as pltpu

F32 = jnp.float32
BF16 = jnp.bfloat16
EPS = 1e-6
ROPE_BASE = 10000.0
GROUP = 32
LANES = 128
HG_HEADS = 4
HG_DIM = 128
MLA_HEADS = 4
NOPE = 128
ROPE = 64
VDIM = 128
QK_PAD = 256
CHUNK = 64
N_EXPERTS = 8
VMEM_LIMIT = 56 * 1024 * 1024

_NT = (((1,), (1,)), ((), ()))
_TN = (((0,), (0,)), ((), ()))


def _dot(a, b):
    return jnp.dot(a, b, preferred_element_type=F32)


def _dot_nt(a, b):
    return lax.dot_general(a, b, _NT, preferred_element_type=F32)


def _dot_tn(a, b):
    return lax.dot_general(a, b, _TN, preferred_element_type=F32)


def _rms(x, w):
    return x * lax.rsqrt(jnp.mean(x * x, axis=-1, keepdims=True) + EPS) * w


def _silu(x):
    return x * jax.nn.sigmoid(x)


def _split_bf16(x):
    hi = x.astype(BF16)
    lo = (x - hi.astype(F32)).astype(BF16)
    return hi, lo


def _pick_tile(n, target, mult):
    best = None
    for t in range(mult, min(n, target) + 1, mult):
        if n % t == 0:
            best = t
    assert best is not None, (n, target, mult)
    return best


def _params(*sem):
    return pltpu.CompilerParams(dimension_semantics=sem, vmem_limit_bytes=VMEM_LIMIT)


def _mod_kernel(c_ref, w_ref, b_ref, o_ref):
    sc = _silu(c_ref[...]).astype(BF16)
    o_ref[0] = _dot(sc, w_ref[0].astype(BF16)) + b_ref[0]


def _modulation(c_all, ada_w, ada_b):
    depth, d, n6 = ada_w.shape
    rows = c_all.shape[0]
    tn = _pick_tile(n6, 1536, LANES)
    return pl.pallas_call(
        _mod_kernel,
        out_shape=jax.ShapeDtypeStruct((depth, rows, n6), F32),
        grid=(depth, n6 // tn),
        in_specs=[
            pl.BlockSpec((rows, d), lambda l, j: (0, 0)),
            pl.BlockSpec((1, d, tn), lambda l, j: (l, 0, j)),
            pl.BlockSpec((1, 1, tn), lambda l, j: (l, 0, j)),
        ],
        out_specs=pl.BlockSpec((1, rows, tn), lambda l, j: (l, 0, j)),
        compiler_params=_params("arbitrary", "arbitrary"),
    )(c_all, ada_w, ada_b.reshape(depth, 1, n6))


def _rope128(g, w128, table):
    t = _rms(g, w128) * table
    return t + pltpu.roll(t, ROPE, 1)


def _front_kernel(x_ref, mod_ref, n1w_ref, win_ref, lb_ref, qaw_ref, wqb_ref, qnw_ref, qrw_ref,
                  kvaw_ref, wkvb_ref, knw_ref, krw_ref, rope_ref,
                  hq_ref, hk_ref, lf_ref, hv_ref, hg_ref, ckv_ref, kpe_ref, q_ref, k_ref, v_ref,
                  h_scr, *, groups, q_lora, kv_lora, scale):
    d = x_ref.shape[1]
    hw = HG_HEADS * HG_DIM
    n1w = n1w_ref[...]
    for g in range(groups):
        rows = slice(g * GROUP, (g + 1) * GROUP)
        shift = mod_ref[g:g + 1, 0:d]
        gain = mod_ref[g:g + 1, d:2 * d]
        h_scr[rows, :] = (_rms(x_ref[rows, :], n1w) * (1.0 + gain) + shift).astype(BF16)
    h = h_scr[...]

    hq_ref[...] = _silu(_dot(h, win_ref[:, 0:hw]))
    one_minus_f = (1.0 - lb_ref[...]) * jax.nn.sigmoid(-_dot(h, win_ref[:, hw:2 * hw]))
    hk_ref[...] = one_minus_f
    lf_ref[...] = jnp.log1p(-one_minus_f)
    hv_ref[...] = _dot(h, win_ref[:, 2 * hw:3 * hw])
    hg_ref[...] = _silu(_dot(h, win_ref[:, 3 * hw:4 * hw]))

    o = 4 * hw
    table = rope_ref[...]
    cq = _rms(_dot(h, win_ref[:, o:o + q_lora]), qaw_ref[...]).astype(BF16)
    qh = _dot(cq, wqb_ref[...])
    nw = MLA_HEADS * NOPE
    for hd in range(MLA_HEADS):
        qn = _rms(qh[:, hd * NOPE:(hd + 1) * NOPE], qnw_ref[...]) * scale
        qr = _rope128(qh[:, nw + hd * LANES:nw + (hd + 1) * LANES], qrw_ref[...], table) * scale
        q_ref[:, hd * QK_PAD:hd * QK_PAD + NOPE] = qn.astype(BF16)
        q_ref[:, hd * QK_PAD + NOPE:(hd + 1) * QK_PAD] = qr.astype(BF16)

    o += q_lora
    ckv = _rms(_dot(h, win_ref[:, o:o + kv_lora]), kvaw_ref[...])
    ckv_ref[...] = ckv
    o += kv_lora
    kp = _rope128(_dot(h, win_ref[:, o:o + LANES]), krw_ref[...], table)
    kpe_ref[...] = kp[:, 0:ROPE]
    lane = lax.broadcasted_iota(jnp.int32, kp.shape, 1)
    kp_pad = jnp.where(lane < ROPE, kp, 0.0).astype(BF16)
    kv = _dot(ckv.astype(BF16), wkvb_ref[...])
    for hd in range(MLA_HEADS):
        c0 = hd * (NOPE + VDIM)
        k_ref[:, hd * QK_PAD:hd * QK_PAD + NOPE] = _rms(kv[:, c0:c0 + NOPE], knw_ref[...]).astype(BF16)
        k_ref[:, hd * QK_PAD + NOPE:(hd + 1) * QK_PAD] = kp_pad
        v_ref[:, hd * VDIM:(hd + 1) * VDIM] = kv[:, c0 + NOPE:c0 + NOPE + VDIM].astype(BF16)


def _front(x, mod_g, n1w, w_in, lb, qaw, wqb, qnw, qrw, kvaw, wkvb, knw, krw, rope, scale):
    n, d = x.shape
    tm = _pick_tile(n, 256, GROUP)
    groups = tm // GROUP
    hw = HG_HEADS * HG_DIM
    q_lora, kv_lora = qaw.shape[1], kvaw.shape[1]
    row = lambda i: (i, 0)
    fixed = lambda i: (0, 0)

    def full(a):
        return pl.BlockSpec(a.shape, fixed)

    f32_out = lambda w: jax.ShapeDtypeStruct((n, w), F32)
    out_shape = [f32_out(hw)] * 5 + [f32_out(kv_lora), f32_out(ROPE),
                                     jax.ShapeDtypeStruct((n, MLA_HEADS * QK_PAD), BF16),
                                     jax.ShapeDtypeStruct((n, MLA_HEADS * QK_PAD), BF16),
                                     jax.ShapeDtypeStruct((n, MLA_HEADS * VDIM), BF16)]
    out_specs = [pl.BlockSpec((tm, s.shape[1]), row) for s in out_shape]
    kern = functools.partial(_front_kernel, groups=groups, q_lora=q_lora, kv_lora=kv_lora, scale=scale)
    return pl.pallas_call(
        kern,
        out_shape=out_shape,
        grid=(n // tm,),
        in_specs=[
            pl.BlockSpec((tm, d), row),
            pl.BlockSpec((groups, 2 * d), row),
            full(n1w), full(w_in), full(lb), full(qaw), full(wqb), full(qnw), full(qrw),
            full(kvaw), full(wkvb), full(knw), full(krw),
            pl.BlockSpec((tm, LANES), row),
        ],
        out_specs=out_specs,
        scratch_shapes=[pltpu.VMEM((tm, d), BF16)],
        compiler_params=_params("arbitrary"),
    )(x, mod_g, n1w, w_in, lb, qaw, wqb, qnw, qrw, kvaw, wkvb, knw, krw, rope)


def _kvup_kernel(ckv_ref, kpe_ref, wkvb_ref, knw_ref, k_ref, v_ref):
    kv = _dot(ckv_ref[...].astype(BF16), wkvb_ref[...])
    kp = kpe_ref[...].astype(BF16)
    for hd in range(MLA_HEADS):
        c0 = hd * (NOPE + VDIM)
        k_ref[:, hd * QK_PAD:hd * QK_PAD + NOPE] = _rms(kv[:, c0:c0 + NOPE], knw_ref[...]).astype(BF16)
        k_ref[:, hd * QK_PAD + NOPE:(hd + 1) * QK_PAD] = kp
        v_ref[:, hd * VDIM:(hd + 1) * VDIM] = kv[:, c0 + NOPE:c0 + NOPE + VDIM].astype(BF16)


def _kv_up(ckv, kpe_pad, wkvb, knw):
    n, r = ckv.shape
    tm = _pick_tile(n, 512, 16)
    row = lambda i: (i, 0)
    fixed = lambda i: (0, 0)
    return pl.pallas_call(
        _kvup_kernel,
        out_shape=[jax.ShapeDtypeStruct((n, MLA_HEADS * QK_PAD), BF16),
                   jax.ShapeDtypeStruct((n, MLA_HEADS * VDIM), BF16)],
        grid=(n // tm,),
        in_specs=[pl.BlockSpec((tm, r), row), pl.BlockSpec((tm, LANES), row),
                  pl.BlockSpec(wkvb.shape, fixed), pl.BlockSpec(knw.shape, fixed)],
        out_specs=[pl.BlockSpec((tm, MLA_HEADS * QK_PAD), row), pl.BlockSpec((tm, MLA_HEADS * VDIM), row)],
        compiler_params=_params("arbitrary"),
    )(ckv, kpe_pad, wkvb, knw)


def _hgrn_consts(length):
    levels = int(math.log2(length))
    assert 1 << levels == length
    coarse = _coarse_levels(levels)
    fine = levels - coarse
    expo = np.zeros(((1 + fine) * length, length), np.float32)
    idx = np.arange(length)
    for t in range(length):
        expo[t, :t + 1] = 1.0
    masks = np.zeros((levels + 1, length, length), np.float32)
    xor = idx[:, None] ^ idx[None, :]
    for m in range(levels):
        block = length >> m
        half = block >> 1
        for t in range(length):
            if m < coarse:
                break
            row = (1 + m - coarse) * length + t
            mid = (t // block) * block + half - 1
            if t > mid:
                expo[row, mid + 1:t + 1] = 1.0
            else:
                expo[row, t + 1:mid + 1] = 1.0
        masks[m] = (idx[:, None] > idx[None, :]) & (xor >= half) & (xor < block)
    masks[levels] = np.eye(length)
    return jnp.asarray(expo, BF16), jnp.asarray(masks, F32)


def _coarse_levels(levels):
    return max(levels - 3, 0)


def _hgrn_kernel(q_ref, k_ref, lf_ref, v_ref, g_ref, s0_ref, expo_ref, mask_ref, onw_ref, alias_ref,
                 o_ref, s_ref, st_scr, *, length, levels, chunks):
    del alias_ref
    c = pl.program_id(1)

    @pl.when(c == 0)
    def _():
        for hd in range(HG_HEADS):
            st_scr[hd] = s0_ref[0, hd].T

    row_idx = lax.broadcasted_iota(jnp.int32, (length, HG_DIM), 0)
    expo = expo_ref[...]
    coarse = _coarse_levels(levels)
    states = [st_scr[hd] for hd in range(HG_HEADS)]
    for ci in range(chunks):
        rows = slice(ci * length, (ci + 1) * length)
        for hd in range(HG_HEADS):
            cols = slice(hd * HG_DIM, (hd + 1) * HG_DIM)
            sums = _dot(expo, lf_ref[rows, cols].astype(BF16))
            cum = sums[0:length]
            from_start = jnp.exp(cum)
            q = q_ref[rows, cols]
            k = k_ref[rows, cols]
            vb = v_ref[rows, cols].astype(BF16)
            st = states[hd]
            o = _dot_nt((q * from_start).astype(BF16), st.astype(BF16))
            scores = _dot_nt(q.astype(BF16), k.astype(BF16)) * mask_ref[levels]
            for m in range(levels):
                half = length >> (m + 1)
                block = 2 * half
                upper = (row_idx & half) != 0
                if m < coarse:
                    mid = jnp.concatenate(
                        [jnp.broadcast_to(cum[j * block + half - 1:j * block + half, :], (block, HG_DIM))
                         for j in range(length // block)], axis=0)
                    expn = jnp.where(upper, cum - mid, mid - cum)
                else:
                    expn = sums[(1 + m - coarse) * length:(2 + m - coarse) * length]
                xm = (jnp.where(upper, q, k) * jnp.exp(expn)).astype(BF16)
                scores = scores + _dot_nt(xm, xm) * mask_ref[m]
            o = o + _dot(scores.astype(BF16), vb)
            k_dec = (k * jnp.exp(cum[length - 1:length] - cum)).astype(BF16)
            states[hd] = st * from_start[length - 1:length] + _dot_tn(vb, k_dec)
            o_ref[rows, cols] = (_rms(o, onw_ref[...]) * g_ref[rows, cols]).astype(BF16)
    for hd in range(HG_HEADS):
        st_scr[hd] = states[hd]

    @pl.when(c == pl.num_programs(1) - 1)
    def _():
        for hd in range(HG_HEADS):
            s_ref[0, hd] = st_scr[hd].T


def _hgrn(hq, hk, lf, hv, hg, s0, onw, o_alias, row0, seq):
    n, hw = hq.shape
    b = s0.shape[0]
    length = 1
    while length * 2 <= min(seq, 128) and seq % (length * 2) == 0:
        length *= 2
    levels = int(math.log2(length))
    chunks = max(c for c in (1, 2, 4) if (seq // length) % c == 0)
    step_rows = chunks * length
    nc = seq // step_rows
    assert row0 % step_rows == 0
    blk0 = row0 // step_rows
    expo, masks = _hgrn_consts(length)
    row = lambda i, c: (blk0 + i * nc + c, 0)
    rows = pl.BlockSpec((step_rows, hw), row)
    state = pl.BlockSpec((1, HG_HEADS, HG_DIM, HG_DIM), lambda i, c: (i, 0, 0, 0))
    kern = functools.partial(_hgrn_kernel, length=length, levels=levels, chunks=chunks)
    return pl.pallas_call(
        kern,
        out_shape=[jax.ShapeDtypeStruct((n, hw), BF16), jax.ShapeDtypeStruct(s0.shape, F32)],
        grid=(b, nc),
        in_specs=[rows, rows, rows, rows, rows, state,
                  pl.BlockSpec(expo.shape, lambda i, c: (0, 0)),
                  pl.BlockSpec(masks.shape, lambda i, c: (0, 0, 0)),
                  pl.BlockSpec(onw.shape, lambda i, c: (0, 0)),
                  pl.BlockSpec(memory_space=pl.ANY)],
        out_specs=[rows, state],
        scratch_shapes=[pltpu.VMEM((HG_HEADS, HG_DIM, HG_DIM), F32)],
        input_output_aliases={9: 0},
        compiler_params=_params("arbitrary", "arbitrary"),
    )(hq, hk, lf, hv, hg, s0, expo, masks, onw, o_alias)


def _attn_prompt_kernel(q_ref, k_ref, v_ref, alias_ref, o_ref, *, tq, nq):
    del alias_ref
    qc = lax.broadcasted_iota(jnp.int32, (tq, tq), 0) // CHUNK
    kc = lax.broadcasted_iota(jnp.int32, (tq, tq), 1) // CHUNK
    visible = kc <= qc
    for i in range(nq):
        own = slice(i * tq, (i + 1) * tq)
        q = q_ref[own, :]
        sd = jnp.where(visible, _dot_nt(q, k_ref[own, :]), -1e30)
        m = jnp.max(sd, axis=-1, keepdims=True)
        if i > 0:
            sp = _dot_nt(q, k_ref[0:i * tq, :])
            m = jnp.maximum(m, jnp.max(sp, axis=-1, keepdims=True))
        pd = jnp.exp(sd - m)
        l = jnp.sum(pd, axis=-1, keepdims=True)
        acc = _dot(pd.astype(BF16), v_ref[own, :])
        if i > 0:
            pp = jnp.exp(sp - m)
            l = l + jnp.sum(pp, axis=-1, keepdims=True)
            acc = acc + _dot(pp.astype(BF16), v_ref[0:i * tq, :])
        o_ref[own, :] = acc / l


def _attn_prompt(q_all, k_all, v_all, o_alias, b, seq):
    n = q_all.shape[0]
    tq = _pick_tile(seq, 256, CHUNK)
    nq = seq // tq
    kern = functools.partial(_attn_prompt_kernel, tq=tq, nq=nq)
    per_seq = lambda w: pl.BlockSpec((seq, w), lambda bi, h: (bi, h))
    return pl.pallas_call(
        kern,
        out_shape=jax.ShapeDtypeStruct((n, MLA_HEADS * VDIM), F32),
        grid=(b, MLA_HEADS),
        in_specs=[per_seq(QK_PAD), per_seq(QK_PAD), per_seq(VDIM), pl.BlockSpec(memory_space=pl.ANY)],
        out_specs=per_seq(VDIM),
        input_output_aliases={3: 0},
        compiler_params=_params("arbitrary", "arbitrary"),
    )(q_all, k_all, v_all, o_alias)


def _attn_sample_kernel(q_ref, kp_ref, vp_ref, kn_ref, vn_ref, alias_ref, o_ref):
    del alias_ref
    q = q_ref[...]
    sp = _dot_nt(q, kp_ref[...])
    sn = _dot_nt(q, kn_ref[...])
    m = jnp.maximum(jnp.max(sp, axis=-1, keepdims=True), jnp.max(sn, axis=-1, keepdims=True))
    pp = jnp.exp(sp - m)
    pn = jnp.exp(sn - m)
    l = jnp.sum(pp, axis=-1, keepdims=True) + jnp.sum(pn, axis=-1, keepdims=True)
    acc = _dot(pp.astype(BF16), vp_ref[...]) + _dot(pn.astype(BF16), vn_ref[...])
    o_ref[...] = acc / l


def _attn_sample(q_all, k_all, v_all, k_past, v_past, o_alias, b, seq, past, row0):
    assert row0 % seq == 0
    blk0 = row0 // seq
    new_q = lambda bi, h: (blk0 + bi, h)
    return pl.pallas_call(
        _attn_sample_kernel,
        out_shape=jax.ShapeDtypeStruct(o_alias.shape, F32),
        grid=(b, MLA_HEADS),
        in_specs=[pl.BlockSpec((seq, QK_PAD), new_q),
                  pl.BlockSpec((past, QK_PAD), lambda bi, h: (bi, h)),
                  pl.BlockSpec((past, VDIM), lambda bi, h: (bi, h)),
                  pl.BlockSpec((seq, QK_PAD), new_q),
                  pl.BlockSpec((seq, VDIM), new_q),
                  pl.BlockSpec(memory_space=pl.ANY)],
        out_specs=pl.BlockSpec((seq, VDIM), new_q),
        input_output_aliases={5: 0},
        compiler_params=_params("arbitrary", "arbitrary"),
    )(q_all, k_past, v_past, k_all, v_all, o_alias)


def _out_kernel(ohg_ref, omla_ref, x_ref, g1_ref, sh2_ref, sc2_ref, monw_ref, wo_ref, n2w_ref,
                x1_ref, h2_ref, *, groups):
    hw = ohg_ref.shape[1]
    om = _rms(omla_ref[...], monw_ref[...]).astype(BF16)
    y = _dot(ohg_ref[...], wo_ref[0:hw, :]) + _dot(om, wo_ref[hw:, :])
    n2w = n2w_ref[...]
    for g in range(groups):
        rows = slice(g * GROUP, (g + 1) * GROUP)
        x1 = x_ref[rows, :] + g1_ref[g:g + 1, :] * y[rows, :]
        x1_ref[rows, :] = x1
        h2 = _rms(x1, n2w) * (1.0 + sc2_ref[g:g + 1, :]) + sh2_ref[g:g + 1, :]
        h2_ref[rows, :] = h2.astype(h2_ref.dtype)


def _out_proj(o_hg, o_mla, x, mod_g, monw, w_o, n2w, h2_dtype):
    n, d = x.shape
    tm = _pick_tile(n, 256, GROUP)
    groups = tm // GROUP
    row = lambda i: (i, 0)
    fixed = lambda i: (0, 0)
    modspec = lambda col: pl.BlockSpec((groups, d), lambda i: (i, col))
    return pl.pallas_call(
        functools.partial(_out_kernel, groups=groups),
        out_shape=[jax.ShapeDtypeStruct((n, d), F32), jax.ShapeDtypeStruct((n, d), h2_dtype)],
        grid=(n // tm,),
        in_specs=[pl.BlockSpec((tm, o_hg.shape[1]), row), pl.BlockSpec((tm, o_mla.shape[1]), row),
                  pl.BlockSpec((tm, d), row), modspec(2), modspec(3), modspec(4),
                  pl.BlockSpec(monw.shape, fixed), pl.BlockSpec(w_o.shape, fixed),
                  pl.BlockSpec(n2w.shape, fixed)],
        out_specs=[pl.BlockSpec((tm, d), row), pl.BlockSpec((tm, d), row)],
        compiler_params=_params("arbitrary"),
    )(o_hg, o_mla, x, mod_g, mod_g, mod_g, monw, w_o, n2w)


def _ffn_kernel(h_ref, wg_ref, wu_ref, wd_ref, x1_ref, g2_ref, o_ref, acc_scr, *, groups):
    j = pl.program_id(1)

    @pl.when(j == 0)
    def _():
        acc_scr[...] = jnp.zeros_like(acc_scr)

    h = h_ref[...]
    a = (_silu(_dot(h, wg_ref[...].astype(BF16))) * _dot(h, wu_ref[...].astype(BF16))).astype(BF16)
    acc_scr[...] += _dot(a, wd_ref[...].astype(BF16))

    @pl.when(j == pl.num_programs(1) - 1)
    def _():
        for g in range(groups):
            rows = slice(g * GROUP, (g + 1) * GROUP)
            o_ref[rows, :] = x1_ref[rows, :] + g2_ref[g:g + 1, :] * acc_scr[rows, :]


def _ffn(h2, wg, wu, wd, x1, mod_g):
    n, d = x1.shape
    f = wg.shape[1]
    tm = _pick_tile(n, 1280, GROUP)
    tf = _pick_tile(f, 512, LANES)
    groups = tm // GROUP
    return pl.pallas_call(
        functools.partial(_ffn_kernel, groups=groups),
        out_shape=jax.ShapeDtypeStruct((n, d), F32),
        grid=(n // tm, f // tf),
        in_specs=[pl.BlockSpec((tm, d), lambda i, j: (i, 0)),
                  pl.BlockSpec((d, tf), lambda i, j: (0, j)),
                  pl.BlockSpec((d, tf), lambda i, j: (0, j)),
                  pl.BlockSpec((tf, d), lambda i, j: (j, 0)),
                  pl.BlockSpec((tm, d), lambda i, j: (i, 0)),
                  pl.BlockSpec((groups, d), lambda i, j: (i, 5))],
        out_specs=pl.BlockSpec((tm, d), lambda i, j: (i, 0)),
        scratch_shapes=[pltpu.VMEM((tm, d), F32)],
        compiler_params=_params("arbitrary", "arbitrary"),
    )(h2, wg, wu, wd, x1, mod_g)


def _router_kernel(h_ref, rhi_ref, rlo_ref, tril_ref, pos_ref, gate_ref, cnt_ref, cnt_scr):
    i = pl.program_id(0)

    @pl.when(i == 0)
    def _():
        cnt_scr[...] = jnp.zeros_like(cnt_scr)

    h_hi, h_lo = _split_bf16(h_ref[...])
    logits = _dot(h_hi, rhi_ref[...]) + (_dot(h_hi, rlo_ref[...]) + _dot(h_lo, rhi_ref[...]))
    lane = lax.broadcasted_iota(jnp.int32, logits.shape, 1).astype(F32)
    logits = jnp.where(lane < N_EXPERTS, logits, -1e30)
    m1 = jnp.max(logits, axis=-1, keepdims=True)
    i1 = jnp.min(jnp.where(logits == m1, lane, float(LANES)), axis=-1, keepdims=True)
    hot1 = lane == i1
    rest = jnp.where(hot1, -1e30, logits)
    m2 = jnp.max(rest, axis=-1, keepdims=True)
    i2 = jnp.min(jnp.where(rest == m2, lane, float(LANES)), axis=-1, keepdims=True)
    hot2 = lane == i2
    e = jnp.exp(m2 - m1)
    gate1 = 1.0 / (1.0 + e)
    gate2 = e / (1.0 + e)
    gate_ref[...] = jnp.where(lane == 0, gate1, jnp.where(lane == 1, gate2, 0.0))
    hot = jnp.where(hot1, 1.0, jnp.where(hot2, 1.0, 0.0))
    before = _dot(tril_ref[...], hot.astype(BF16)) + cnt_scr[0:1, :]
    rank1 = jnp.sum(jnp.where(hot1, before, 0.0), axis=-1, keepdims=True)
    rank2 = jnp.sum(jnp.where(hot2, before, 0.0), axis=-1, keepdims=True)
    info = jnp.where(lane == 0, rank1, jnp.where(lane == 1, rank2,
                     jnp.where(lane == 2, i1, jnp.where(lane == 3, i2, 0.0))))
    pos_ref[...] = info.astype(jnp.int32)
    total = cnt_scr[0:1, :] + jnp.sum(hot, axis=0, keepdims=True)
    cnt_scr[...] = jnp.broadcast_to(total, cnt_scr.shape)
    cnt_ref[...] = jnp.broadcast_to(total, cnt_ref.shape).astype(jnp.int32)


def _router(h2, router):
    n, d = h2.shape
    tm = _pick_tile(n, 1280, GROUP)
    r_pad = jnp.zeros((d, LANES), F32).at[:, :N_EXPERTS].set(router)
    r_hi = r_pad.astype(BF16)
    r_lo = (r_pad - r_hi.astype(F32)).astype(BF16)
    tril = jnp.asarray(np.tril(np.ones((tm, tm), np.float32), -1), BF16)
    row = lambda i: (i, 0)
    fixed = lambda i: (0, 0)
    return pl.pallas_call(
        _router_kernel,
        out_shape=[jax.ShapeDtypeStruct((n, LANES), jnp.int32), jax.ShapeDtypeStruct((n, LANES), F32),
                   jax.ShapeDtypeStruct((8, LANES), jnp.int32)],
        grid=(n // tm,),
        in_specs=[pl.BlockSpec((tm, d), row), pl.BlockSpec((d, LANES), fixed),
                  pl.BlockSpec((d, LANES), fixed), pl.BlockSpec((tm, tm), fixed)],
        out_specs=[pl.BlockSpec((tm, LANES), row), pl.BlockSpec((tm, LANES), row),
                   pl.BlockSpec((8, LANES), fixed)],
        scratch_shapes=[pltpu.VMEM((8, LANES), F32)],
        compiler_params=_params("arbitrary"),
    )(h2, r_hi, r_lo, tril)


def _row_copy(src, src_row, dst, dst_row, sem):
    return pltpu.make_async_copy(src.at[pl.ds(src_row, 1), :], dst.at[pl.ds(dst_row, 1), :], sem)


def _scatter_kernel(cnt_ref, off_ref, nu_ref, pos_ref, h_ref, xs_ref, zero_scr, sem, *, tm, tile, n_tiles):
    i = pl.program_id(0)

    def start(r, _):
        _row_copy(h_ref, r, xs_ref, pos_ref[0, 0, 2 * r], sem).start()
        _row_copy(h_ref, r, xs_ref, pos_ref[0, 0, 2 * r + 1], sem).start()
        return 0

    lax.fori_loop(0, tm, start, 0, unroll=16)
    for _ in range(2 * tm):
        _row_copy(h_ref, 0, xs_ref, 0, sem).wait()

    @pl.when(i == pl.num_programs(0) - 1)
    def _():
        zero_scr[...] = jnp.zeros_like(zero_scr)
        for e in range(N_EXPERTS):
            cnt = cnt_ref[e]
            end = ((cnt + tile - 1) // tile) * tile

            def fill(r, _):
                _row_copy(zero_scr, 0, xs_ref, off_ref[e] + r, sem).start()
                return 0

            def fill_wait(r, _):
                _row_copy(zero_scr, 0, xs_ref, 0, sem).wait()
                return 0

            lax.fori_loop(cnt, end, fill, 0)
            lax.fori_loop(cnt, end, fill_wait, 0)

        def fill_tile(t, _):
            cp = pltpu.make_async_copy(zero_scr, xs_ref.at[pl.ds(pl.multiple_of(t * tile, tile), tile), :], sem)
            cp.start()
            cp.wait()
            return 0

        lax.fori_loop(nu_ref[0], n_tiles, fill_tile, 0)


def _scatter(h2, pos2, counts, offsets, n_used, tile, n_tiles):
    n, d = h2.shape
    tm = pos2.shape[2] // 2
    grid_spec = pltpu.PrefetchScalarGridSpec(
        num_scalar_prefetch=3,
        grid=(n // tm,),
        in_specs=[pl.BlockSpec((1, 1, 2 * tm), lambda i, *_: (i, 0, 0), memory_space=pltpu.SMEM),
                  pl.BlockSpec((tm, d), lambda i, *_: (i, 0))],
        out_specs=pl.BlockSpec(memory_space=pl.ANY),
        scratch_shapes=[pltpu.VMEM((tile, d), F32), pltpu.SemaphoreType.DMA(())],
    )
    return pl.pallas_call(
        functools.partial(_scatter_kernel, tm=tm, tile=tile, n_tiles=n_tiles),
        out_shape=jax.ShapeDtypeStruct((n_tiles * tile, d), F32),
        grid_spec=grid_spec,
        compiler_params=_params("arbitrary"),
    )(counts, offsets, n_used, pos2, h2)


def _moe_kernel(te_ref, nu_ref, x_ref, wg_ref, wu_ref, wd_ref, y_ref, xb_scr, acc_scr):
    i = pl.program_id(0)
    j = pl.program_id(1)

    @pl.when((i >= nu_ref[0]) & (j == 0))
    def _():
        y_ref[...] = jnp.zeros_like(y_ref)

    @pl.when(i < nu_ref[0])
    def _():
        @pl.when(j == 0)
        def _():
            xb_scr[...] = x_ref[...].astype(BF16)
            acc_scr[...] = jnp.zeros_like(acc_scr)

        x = xb_scr[...]
        a = (_silu(_dot(x, wg_ref[0].astype(BF16))) * _dot(x, wu_ref[0].astype(BF16))).astype(BF16)
        acc_scr[...] += _dot(a, wd_ref[0].astype(BF16))

        @pl.when(j == pl.num_programs(1) - 1)
        def _():
            y_ref[...] = acc_scr[...]


def _moe(xs, wg, wu, wd, tile_e, n_used, tm):
    rows, d = xs.shape
    f = wg.shape[2]
    tf = _pick_tile(f, 896, LANES)
    nf = f // tf

    def fcol(i, j, nu):
        return jnp.where(i < nu[0], j, nf - 1)

    grid_spec = pltpu.PrefetchScalarGridSpec(
        num_scalar_prefetch=2,
        grid=(rows // tm, nf),
        in_specs=[pl.BlockSpec((tm, d), lambda i, j, te, nu: (i, 0)),
                  pl.BlockSpec((1, d, tf), lambda i, j, te, nu: (te[i], 0, fcol(i, j, nu))),
                  pl.BlockSpec((1, d, tf), lambda i, j, te, nu: (te[i], 0, fcol(i, j, nu))),
                  pl.BlockSpec((1, tf, d), lambda i, j, te, nu: (te[i], fcol(i, j, nu), 0))],
        out_specs=pl.BlockSpec((tm, d), lambda i, j, te, nu: (i, 0)),
        scratch_shapes=[pltpu.VMEM((tm, d), BF16), pltpu.VMEM((tm, d), F32)],
    )
    return pl.pallas_call(
        _moe_kernel,
        out_shape=jax.ShapeDtypeStruct((rows, d), F32),
        grid_spec=grid_spec,
        compiler_params=_params("arbitrary", "arbitrary"),
    )(tile_e, n_used, xs, wg, wu, wd)


def _combine_kernel(pos_ref, gate_ref, x1_ref, g2_ref, y_ref, o_ref, buf_scr, sem, *, tm, groups):
    def start(r, _):
        pltpu.make_async_copy(y_ref.at[pl.ds(pos_ref[0, 0, 2 * r], 1), :],
                              buf_scr.at[0, pl.ds(r, 1), :], sem).start()
        pltpu.make_async_copy(y_ref.at[pl.ds(pos_ref[0, 0, 2 * r + 1], 1), :],
                              buf_scr.at[1, pl.ds(r, 1), :], sem).start()
        return 0

    lax.fori_loop(0, tm, start, 0, unroll=16)
    for _ in range(2 * tm):
        pltpu.make_async_copy(y_ref.at[pl.ds(0, 1), :], buf_scr.at[0, pl.ds(0, 1), :], sem).wait()
    for g in range(groups):
        rows = slice(g * GROUP, (g + 1) * GROUP)
        f = gate_ref[rows, 0:1] * buf_scr[0, rows, :] + gate_ref[rows, 1:2] * buf_scr[1, rows, :]
        o_ref[rows, :] = x1_ref[rows, :] + g2_ref[g:g + 1, :] * f


def _combine(y, pos2, gates, x1, mod_g):
    n, d = x1.shape
    tm = pos2.shape[2] // 2
    groups = tm // GROUP
    return pl.pallas_call(
        functools.partial(_combine_kernel, tm=tm, groups=groups),
        out_shape=jax.ShapeDtypeStruct((n, d), F32),
        grid=(n // tm,),
        in_specs=[pl.BlockSpec((1, 1, 2 * tm), lambda i: (i, 0, 0), memory_space=pltpu.SMEM),
                  pl.BlockSpec((tm, LANES), lambda i: (i, 0)),
                  pl.BlockSpec((tm, d), lambda i: (i, 0)),
                  pl.BlockSpec((groups, d), lambda i: (i, 5)),
                  pl.BlockSpec(memory_space=pl.ANY)],
        out_specs=pl.BlockSpec((tm, d), lambda i: (i, 0)),
        scratch_shapes=[pltpu.VMEM((2, tm, d), F32), pltpu.SemaphoreType.DMA(())],
        compiler_params=_params("arbitrary"),
    )(pos2, gates, x1, mod_g, y)


def _routed_ffn(h2, router, wg, wu, wd, x1, mod_g):
    n, d = x1.shape
    tile = _pick_tile(n, 640, GROUP)
    n_tiles = (2 * n) // tile + N_EXPERTS
    info, gates, counts = _router(h2, router)
    counts = counts[0, :N_EXPERTS]
    tiles_per = (counts + tile - 1) // tile
    ends = jnp.cumsum(tiles_per)
    offsets = ((ends - tiles_per) * tile).astype(jnp.int32)
    n_used = ends[-1:].astype(jnp.int32)
    t = jnp.minimum(jnp.arange(n_tiles, dtype=jnp.int32), n_used - 1)
    tile_e = jnp.sum(t[:, None] >= ends[None, :], axis=1).astype(jnp.int32)
    tm = _pick_tile(n, 256, GROUP)
    pos2 = (offsets[info[:, 2:4]] + info[:, 0:2]).reshape(n // tm, 1, 2 * tm)
    xs = _scatter(h2, pos2, counts, offsets, n_used, tile, n_tiles)
    y = _moe(xs, wg, wu, wd, tile_e, n_used, tile)
    return _combine(y, pos2, gates, x1, mod_g)


def _swap_halves(a):
    half = a.shape[-1] // 2
    return jnp.concatenate([a[..., half:], a[..., :half]], axis=-1)


def kernel(x_prompt, x_sample, c_prompt, c_sample, state_hgrn, cache_ckv, cache_kpe, ada_w, ada_b, norm1_w, norm2_w, w_in, hg_lower_bounds, hg_onorm_w, mla_qa_norm_w, mla_wqb, mla_kva_norm_w, mla_wkvb, mla_qn_nope_w, mla_qn_rope_w, mla_kn_nope_w, mla_kn_rope_w, mla_onorm_w, w_o, ffn_w_gate, ffn_w_up, ffn_w_down, moe_router, moe_w_gate, moe_w_up, moe_w_down):
    bp, sp, d = x_prompt.shape
    bs, ss, _ = x_sample.shape
    depth = ada_w.shape[0]
    past = cache_ckv.shape[2]
    n_p, n_s = bp * sp, bs * ss
    n = n_p + n_s
    hw = HG_HEADS * HG_DIM
    q_lora = mla_qa_norm_w.shape[1]
    kv_lora = mla_kva_norm_w.shape[1]
    assert sp % GROUP == 0 and ss % GROUP == 0 and n_p % ss == 0
    scale = float((NOPE + ROPE) ** -0.5)

    x = jnp.concatenate([x_prompt.reshape(n_p, d), x_sample.reshape(n_s, d)], axis=0)
    mod = _modulation(jnp.concatenate([c_prompt, c_sample], axis=0), ada_w, ada_b)
    mod_g = jnp.concatenate([jnp.repeat(mod[:, :bp], sp // GROUP, axis=1),
                             jnp.repeat(mod[:, bp:], ss // GROUP, axis=1)], axis=1)

    lw = jax.nn.softmax(hg_lower_bounds.astype(F32), axis=0)
    lower = jnp.clip(jnp.cumsum(lw, axis=0) - lw[0], 0.0, 1.0)

    half = ROPE // 2
    inv_freq = ROPE_BASE ** (-jnp.arange(half, dtype=F32) / half)

    def rope_rows(pos):
        ang = pos.astype(F32)[:, None] * inv_freq[None, :]
        return jnp.concatenate([jnp.cos(ang), jnp.cos(ang), -jnp.sin(ang), jnp.sin(ang)], axis=-1)

    rope = jnp.concatenate([jnp.tile(rope_rows(jnp.arange(sp)), (bp, 1)),
                            jnp.tile(rope_rows(past + jnp.arange(ss)), (bs, 1))], axis=0)

    row2 = lambda a: a.reshape(1, -1)
    pair = lambda w: row2(jnp.concatenate([w, _swap_halves(w)]))
    zeros_state = jnp.zeros((bp,) + state_hgrn.shape[2:], F32)

    new_s_p, new_s_s, new_ckv, new_kpe = [], [], [], []
    for l in range(depth):
        kpe_cols = w_in[l][:, -ROPE:]
        w_in_l = jnp.concatenate([w_in[l], _swap_halves(kpe_cols)], axis=1).astype(BF16)
        wqb3 = mla_wqb[l].reshape(q_lora, MLA_HEADS, NOPE + ROPE)
        q_rope = wqb3[:, :, NOPE:]
        wqb_l = jnp.concatenate(
            [wqb3[:, :, :NOPE].reshape(q_lora, MLA_HEADS * NOPE),
             jnp.concatenate([q_rope, _swap_halves(q_rope)], axis=-1).reshape(q_lora, MLA_HEADS * LANES)],
            axis=1).astype(BF16)
        wkvb_l = mla_wkvb[l].astype(BF16)
        knw = row2(mla_kn_nope_w[l])

        hq, hk, lf, hv, hg, ckv, kpe, q_all, k_all, v_all = _front(
            x, mod_g[l], row2(norm1_w[l]), w_in_l, row2(lower[l]), row2(mla_qa_norm_w[l]), wqb_l,
            row2(mla_qn_nope_w[l]), pair(mla_qn_rope_w[l]), row2(mla_kva_norm_w[l]), wkvb_l, knw,
            pair(mla_kn_rope_w[l]), rope, scale)

        onw = row2(hg_onorm_w[l])
        o_hg = jnp.zeros((n, hw), BF16)
        o_hg, s_p = _hgrn(hq, hk, lf, hv, hg, zeros_state, onw, o_hg, 0, sp)
        o_hg, s_s = _hgrn(hq, hk, lf, hv, hg, state_hgrn[l].astype(F32), onw, o_hg, n_p, ss)

        kpe_past = jnp.pad(cache_kpe[l].reshape(bs * past, ROPE).astype(BF16), ((0, 0), (0, LANES - ROPE)))
        k_past, v_past = _kv_up(cache_ckv[l].reshape(bs * past, kv_lora), kpe_past, wkvb_l, knw)
        o_mla = _attn_prompt(q_all, k_all, v_all, jnp.zeros((n, MLA_HEADS * VDIM), F32), bp, sp)
        o_mla = _attn_sample(q_all, k_all, v_all, k_past, v_past, o_mla, bs, ss, past, n_p)

        moe_layer = l % 2 == 1
        x1, h2 = _out_proj(o_hg, o_mla, x, mod_g[l], row2(mla_onorm_w[l]), w_o[l].astype(BF16),
                           row2(norm2_w[l]), F32 if moe_layer else BF16)
        j = l // 2
        if moe_layer:
            x = _routed_ffn(h2, moe_router[j], moe_w_gate[j], moe_w_up[j], moe_w_down[j], x1, mod_g[l])
        else:
            x = _ffn(h2, ffn_w_gate[j], ffn_w_up[j], ffn_w_down[j], x1, mod_g[l])

        new_s_p.append(s_p)
        new_s_s.append(s_s)
        new_ckv.append(ckv)
        new_kpe.append(kpe)

    ckv_all = jnp.stack(new_ckv)
    kpe_all = jnp.stack(new_kpe)
    return (x[:n_p].reshape(bp, sp, d), x[n_p:].reshape(bs, ss, d),
            jnp.stack(new_s_p), ckv_all[:, :n_p].reshape(depth, bp, sp, kv_lora),
            kpe_all[:, :n_p].reshape(depth, bp, sp, ROPE),
            jnp.stack(new_s_s), ckv_all[:, n_p:].reshape(depth, bs, ss, kv_lora),
            kpe_all[:, n_p:].reshape(depth, bs, ss, ROPE))
```

```python
import functools
import math

import numpy as np
import jax
import jax.numpy as jnp
from jax import lax
from jax.experimental import pallas as pl
from jax.experimental.pallas import tpu as pltpu

F32 = jnp.float32
BF16 = jnp.bfloat16
EPS = 1e-6
ROPE_BASE = 10000.0
GROUP = 32
LANES = 128
HG_HEADS = 4
HG_DIM = 128
MLA_HEADS = 4
NOPE = 128
ROPE = 64
VDIM = 128
QK_PAD = 256
CHUNK = 64
N_EXPERTS = 8
VMEM_LIMIT = 56 * 1024 * 1024

_NT = (((1,), (1,)), ((), ()))
_TN = (((0,), (0,)), ((), ()))


def _dot(a, b):
    return jnp.dot(a, b, preferred_element_type=F32)


def _dot_nt(a, b):
    return lax.dot_general(a, b, _NT, preferred_element_type=F32)


def _dot_tn(a, b):
    return lax.dot_general(a, b, _TN, preferred_element_type=F32)


def _rms(x, w):
    return x * lax.rsqrt(jnp.mean(x * x, axis=-1, keepdims=True) + EPS) * w


def _silu(x):
    return x * jax.nn.sigmoid(x)


def _split_bf16(x):
    hi = x.astype(BF16)
    lo = (x - hi.astype(F32)).astype(BF16)
    return hi, lo


def _pick_tile(n, target, mult):
    best = None
    for t in range(mult, min(n, target) + 1, mult):
        if n % t == 0:
            best = t
    assert best is not None, (n, target, mult)
    return best


def _params(*sem):
    return pltpu.CompilerParams(dimension_semantics=sem, vmem_limit_bytes=VMEM_LIMIT)


def _mod_kernel(c_ref, w_ref, b_ref, o_ref):
    sc = _silu(c_ref[...]).astype(BF16)
    o_ref[0] = _dot(sc, w_ref[0].astype(BF16)) + b_ref[0]


def _modulation(c_all, ada_w, ada_b):
    depth, d, n6 = ada_w.shape
    rows = c_all.shape[0]
    tn = _pick_tile(n6, 1536, LANES)
    return pl.pallas_call(
        _mod_kernel,
        out_shape=jax.ShapeDtypeStruct((depth, rows, n6), F32),
        grid=(depth, n6 // tn),
        in_specs=[
            pl.BlockSpec((rows, d), lambda l, j: (0, 0)),
            pl.BlockSpec((1, d, tn), lambda l, j: (l, 0, j)),
            pl.BlockSpec((1, 1, tn), lambda l, j: (l, 0, j)),
        ],
        out_specs=pl.BlockSpec((1, rows, tn), lambda l, j: (l, 0, j)),
        compiler_params=_params("arbitrary", "arbitrary"),
    )(c_all, ada_w, ada_b.reshape(depth, 1, n6))


def _rope128(g, w128, table):
    t = _rms(g, w128) * table
    return t + pltpu.roll(t, ROPE, 1)


def _front_kernel(x_ref, mod_ref, n1w_ref, win_ref, lb_ref, qaw_ref, wqb_ref, qnw_ref, qrw_ref,
                  kvaw_ref, wkvb_ref, knw_ref, krw_ref, rope_ref,
                  hq_ref, hk_ref, lf_ref, hv_ref, hg_ref, ckv_ref, kpe_ref, q_ref, k_ref, v_ref,
                  h_scr, *, groups, q_lora, kv_lora, scale):
    d = x_ref.shape[1]
    hw = HG_HEADS * HG_DIM
    n1w = n1w_ref[...]
    for g in range(groups):
        rows = slice(g * GROUP, (g + 1) * GROUP)
        shift = mod_ref[g:g + 1, 0:d]
        gain = mod_ref[g:g + 1, d:2 * d]
        h_scr[rows, :] = (_rms(x_ref[rows, :], n1w) * (1.0 + gain) + shift).astype(BF16)
    h = h_scr[...]

    hq_ref[...] = _silu(_dot(h, win_ref[:, 0:hw]))
    one_minus_f = (1.0 - lb_ref[...]) * jax.nn.sigmoid(-_dot(h, win_ref[:, hw:2 * hw]))
    hk_ref[...] = one_minus_f
    lf_ref[...] = jnp.log1p(-one_minus_f)
    hv_ref[...] = _dot(h, win_ref[:, 2 * hw:3 * hw])
    hg_ref[...] = _silu(_dot(h, win_ref[:, 3 * hw:4 * hw]))

    o = 4 * hw
    table = rope_ref[...]
    cq = _rms(_dot(h, win_ref[:, o:o + q_lora]), qaw_ref[...]).astype(BF16)
    qh = _dot(cq, wqb_ref[...])
    nw = MLA_HEADS * NOPE
    for hd in range(MLA_HEADS):
        qn = _rms(qh[:, hd * NOPE:(hd + 1) * NOPE], qnw_ref[...]) * scale
        qr = _rope128(qh[:, nw + hd * LANES:nw + (hd + 1) * LANES], qrw_ref[...], table) * scale
        q_ref[:, hd * QK_PAD:hd * QK_PAD + NOPE] = qn.astype(BF16)
        q_ref[:, hd * QK_PAD + NOPE:(hd + 1) * QK_PAD] = qr.astype(BF16)

    o += q_lora
    ckv = _rms(_dot(h, win_ref[:, o:o + kv_lora]), kvaw_ref[...])
    ckv_ref[...] = ckv
    o += kv_lora
    kp = _rope128(_dot(h, win_ref[:, o:o + LANES]), krw_ref[...], table)
    kpe_ref[...] = kp[:, 0:ROPE]
    lane = lax.broadcasted_iota(jnp.int32, kp.shape, 1)
    kp_pad = jnp.where(lane < ROPE, kp, 0.0).astype(BF16)
    kv = _dot(ckv.astype(BF16), wkvb_ref[...])
    for hd in range(MLA_HEADS):
        c0 = hd * (NOPE + VDIM)
        k_ref[:, hd * QK_PAD:hd * QK_PAD + NOPE] = _rms(kv[:, c0:c0 + NOPE], knw_ref[...]).astype(BF16)
        k_ref[:, hd * QK_PAD + NOPE:(hd + 1) * QK_PAD] = kp_pad
        v_ref[:, hd * VDIM:(hd + 1) * VDIM] = kv[:, c0 + NOPE:c0 + NOPE + VDIM].astype(BF16)


def _front(x, mod_g, n1w, w_in, lb, qaw, wqb, qnw, qrw, kvaw, wkvb, knw, krw, rope, scale):
    n, d = x.shape
    tm = _pick_tile(n, 256, GROUP)
    groups = tm // GROUP
    hw = HG_HEADS * HG_DIM
    q_lora, kv_lora = qaw.shape[1], kvaw.shape[1]
    row = lambda i: (i, 0)
    fixed = lambda i: (0, 0)

    def full(a):
        return pl.BlockSpec(a.shape, fixed)

    f32_out = lambda w: jax.ShapeDtypeStruct((n, w), F32)
    out_shape = [f32_out(hw)] * 5 + [f32_out(kv_lora), f32_out(ROPE),
                                     jax.ShapeDtypeStruct((n, MLA_HEADS * QK_PAD), BF16),
                                     jax.ShapeDtypeStruct((n, MLA_HEADS * QK_PAD), BF16),
                                     jax.ShapeDtypeStruct((n, MLA_HEADS * VDIM), BF16)]
    out_specs = [pl.BlockSpec((tm, s.shape[1]), row) for s in out_shape]
    kern = functools.partial(_front_kernel, groups=groups, q_lora=q_lora, kv_lora=kv_lora, scale=scale)
    return pl.pallas_call(
        kern,
        out_shape=out_shape,
        grid=(n // tm,),
        in_specs=[
            pl.BlockSpec((tm, d), row),
            pl.BlockSpec((groups, 2 * d), row),
            full(n1w), full(w_in), full(lb), full(qaw), full(wqb), full(qnw), full(qrw),
            full(kvaw), full(wkvb), full(knw), full(krw),
            pl.BlockSpec((tm, LANES), row),
        ],
        out_specs=out_specs,
        scratch_shapes=[pltpu.VMEM((tm, d), BF16)],
        compiler_params=_params("arbitrary"),
    )(x, mod_g, n1w, w_in, lb, qaw, wqb, qnw, qrw, kvaw, wkvb, knw, krw, rope)


def _kvup_kernel(ckv_ref, kpe_ref, wkvb_ref, knw_ref, k_ref, v_ref):
    kv = _dot(ckv_ref[...].astype(BF16), wkvb_ref[...])
    kp = kpe_ref[...].astype(BF16)
    for hd in range(MLA_HEADS):
        c0 = hd * (NOPE + VDIM)
        k_ref[:, hd * QK_PAD:hd * QK_PAD + NOPE] = _rms(kv[:, c0:c0 + NOPE], knw_ref[...]).astype(BF16)
        k_ref[:, hd * QK_PAD + NOPE:(hd + 1) * QK_PAD] = kp
        v_ref[:, hd * VDIM:(hd + 1) * VDIM] = kv[:, c0 + NOPE:c0 + NOPE + VDIM].astype(BF16)


def _kv_up(ckv, kpe_pad, wkvb, knw):
    n, r = ckv.shape
    tm = _pick_tile(n, 512, 16)
    row = lambda i: (i, 0)
    fixed = lambda i: (0, 0)
    return pl.pallas_call(
        _kvup_kernel,
        out_shape=[jax.ShapeDtypeStruct((n, MLA_HEADS * QK_PAD), BF16),
                   jax.ShapeDtypeStruct((n, MLA_HEADS * VDIM), BF16)],
        grid=(n // tm,),
        in_specs=[pl.BlockSpec((tm, r), row), pl.BlockSpec((tm, LANES), row),
                  pl.BlockSpec(wkvb.shape, fixed), pl.BlockSpec(knw.shape, fixed)],
        out_specs=[pl.BlockSpec((tm, MLA_HEADS * QK_PAD), row), pl.BlockSpec((tm, MLA_HEADS * VDIM), row)],
        compiler_params=_params("arbitrary"),
    )(ckv, kpe_pad, wkvb, knw)


def _hgrn_consts(length):
    levels = int(math.log2(length))
    assert 1 << levels == length
    coarse = _coarse_levels(levels)
    fine = levels - coarse
    expo = np.zeros(((1 + fine) * length, length), np.float32)
    idx = np.arange(length)
    for t in range(length):
        expo[t, :t + 1] = 1.0
    masks = np.zeros((levels + 1, length, length), np.float32)
    xor = idx[:, None] ^ idx[None, :]
    for m in range(levels):
        block = length >> m
        half = block >> 1
        for t in range(length):
            if m < coarse:
                break
            row = (1 + m - coarse) * length + t
            mid = (t // block) * block + half - 1
            if t > mid:
                expo[row, mid + 1:t + 1] = 1.0
            else:
                expo[row, t + 1:mid + 1] = 1.0
        masks[m] = (idx[:, None] > idx[None, :]) & (xor >= half) & (xor < block)
    masks[levels] = np.eye(length)
    return jnp.asarray(expo, BF16), jnp.asarray(masks, F32)


def _coarse_levels(levels):
    return max(levels - 3, 0)


def _hgrn_kernel(q_ref, k_ref, lf_ref, v_ref, g_ref, s0_ref, expo_ref, mask_ref, onw_ref, alias_ref,
                 o_ref, s_ref, st_scr, *, length, levels, chunks):
    del alias_ref
    c = pl.program_id(1)

    @pl.when(c == 0)
    def _():
        for hd in range(HG_HEADS):
            st_scr[hd] = s0_ref[0, hd].T

    row_idx = lax.broadcasted_iota(jnp.int32, (length, HG_DIM), 0)
    expo = expo_ref[...]
    coarse = _coarse_levels(levels)
    states = [st_scr[hd] for hd in range(HG_HEADS)]
    for ci in range(chunks):
        rows = slice(ci * length, (ci + 1) * length)
        for hd in range(HG_HEADS):
            cols = slice(hd * HG_DIM, (hd + 1) * HG_DIM)
            sums = _dot(expo, lf_ref[rows, cols].astype(BF16))
            cum = sums[0:length]
            from_start = jnp.exp(cum)
            q = q_ref[rows, cols]
            k = k_ref[rows, cols]
            vb = v_ref[rows, cols].astype(BF16)
            st = states[hd]
            o = _dot_nt((q * from_start).astype(BF16), st.astype(BF16))
            scores = _dot_nt(q.astype(BF16), k.astype(BF16)) * mask_ref[levels]
            for m in range(levels):
                half = length >> (m + 1)
                block = 2 * half
                upper = (row_idx & half) != 0
                if m < coarse:
                    mid = jnp.concatenate(
                        [jnp.broadcast_to(cum[j * block + half - 1:j * block + half, :], (block, HG_DIM))
                         for j in range(length // block)], axis=0)
                    expn = jnp.where(upper, cum - mid, mid - cum)
                else:
                    expn = sums[(1 + m - coarse) * length:(2 + m - coarse) * length]
                xm = (jnp.where(upper, q, k) * jnp.exp(expn)).astype(BF16)
                scores = scores + _dot_nt(xm, xm) * mask_ref[m]
            o = o + _dot(scores.astype(BF16), vb)
            k_dec = (k * jnp.exp(cum[length - 1:length] - cum)).astype(BF16)
            states[hd] = st * from_start[length - 1:length] + _dot_tn(vb, k_dec)
            o_ref[rows, cols] = (_rms(o, onw_ref[...]) * g_ref[rows, cols]).astype(BF16)
    for hd in range(HG_HEADS):
        st_scr[hd] = states[hd]

    @pl.when(c == pl.num_programs(1) - 1)
    def _():
        for hd in range(HG_HEADS):
            s_ref[0, hd] = st_scr[hd].T


def _hgrn(hq, hk, lf, hv, hg, s0, onw, o_alias, row0, seq):
    n, hw = hq.shape
    b = s0.shape[0]
    length = 1
    while length * 2 <= min(seq, 128) and seq % (length * 2) == 0:
        length *= 2
    levels = int(math.log2(length))
    chunks = max(c for c in (1, 2, 4) if (seq // length) % c == 0)
    step_rows = chunks * length
    nc = seq // step_rows
    assert row0 % step_rows == 0
    blk0 = row0 // step_rows
    expo, masks = _hgrn_consts(length)
    row = lambda i, c: (blk0 + i * nc + c, 0)
    rows = pl.BlockSpec((step_rows, hw), row)
    state = pl.BlockSpec((1, HG_HEADS, HG_DIM, HG_DIM), lambda i, c: (i, 0, 0, 0))
    kern = functools.partial(_hgrn_kernel, length=length, levels=levels, chunks=chunks)
    return pl.pallas_call(
        kern,
        out_shape=[jax.ShapeDtypeStruct((n, hw), BF16), jax.ShapeDtypeStruct(s0.shape, F32)],
        grid=(b, nc),
        in_specs=[rows, rows, rows, rows, rows, state,
                  pl.BlockSpec(expo.shape, lambda i, c: (0, 0)),
                  pl.BlockSpec(masks.shape, lambda i, c: (0, 0, 0)),
                  pl.BlockSpec(onw.shape, lambda i, c: (0, 0)),
                  pl.BlockSpec(memory_space=pl.ANY)],
        out_specs=[rows, state],
        scratch_shapes=[pltpu.VMEM((HG_HEADS, HG_DIM, HG_DIM), F32)],
        input_output_aliases={9: 0},
        compiler_params=_params("arbitrary", "arbitrary"),
    )(hq, hk, lf, hv, hg, s0, expo, masks, onw, o_alias)


def _attn_prompt_kernel(q_ref, k_ref, v_ref, alias_ref, o_ref, *, tq, nq):
    del alias_ref
    qc = lax.broadcasted_iota(jnp.int32, (tq, tq), 0) // CHUNK
    kc = lax.broadcasted_iota(jnp.int32, (tq, tq), 1) // CHUNK
    visible = kc <= qc
    for i in range(nq):
        own = slice(i * tq, (i + 1) * tq)
        q = q_ref[own, :]
        sd = jnp.where(visible, _dot_nt(q, k_ref[own, :]), -1e30)
        m = jnp.max(sd, axis=-1, keepdims=True)
        if i > 0:
            sp = _dot_nt(q, k_ref[0:i * tq, :])
            m = jnp.maximum(m, jnp.max(sp, axis=-1, keepdims=True))
        pd = jnp.exp(sd - m)
        l = jnp.sum(pd, axis=-1, keepdims=True)
        acc = _dot(pd.astype(BF16), v_ref[own, :])
        if i > 0:
            pp = jnp.exp(sp - m)
            l = l + jnp.sum(pp, axis=-1, keepdims=True)
            acc = acc + _dot(pp.astype(BF16), v_ref[0:i * tq, :])
        o_ref[own, :] = acc / l


def _attn_prompt(q_all, k_all, v_all, o_alias, b, seq):
    n = q_all.shape[0]
    tq = _pick_tile(seq, 256, CHUNK)
    nq = seq // tq
    kern = functools.partial(_attn_prompt_kernel, tq=tq, nq=nq)
    per_seq = lambda w: pl.BlockSpec((seq, w), lambda bi, h: (bi, h))
    return pl.pallas_call(
        kern,
        out_shape=jax.ShapeDtypeStruct((n, MLA_HEADS * VDIM), F32),
        grid=(b, MLA_HEADS),
        in_specs=[per_seq(QK_PAD), per_seq(QK_PAD), per_seq(VDIM), pl.BlockSpec(memory_space=pl.ANY)],
        out_specs=per_seq(VDIM),
        input_output_aliases={3: 0},
        compiler_params=_params("arbitrary", "arbitrary"),
    )(q_all, k_all, v_all, o_alias)


def _attn_sample_kernel(q_ref, kp_ref, vp_ref, kn_ref, vn_ref, alias_ref, o_ref):
    del alias_ref
    q = q_ref[...]
    sp = _dot_nt(q, kp_ref[...])
    sn = _dot_nt(q, kn_ref[...])
    m = jnp.maximum(jnp.max(sp, axis=-1, keepdims=True), jnp.max(sn, axis=-1, keepdims=True))
    pp = jnp.exp(sp - m)
    pn = jnp.exp(sn - m)
    l = jnp.sum(pp, axis=-1, keepdims=True) + jnp.sum(pn, axis=-1, keepdims=True)
    acc = _dot(pp.astype(BF16), vp_ref[...]) + _dot(pn.astype(BF16), vn_ref[...])
    o_ref[...] = acc / l


def _attn_sample(q_all, k_all, v_all, k_past, v_past, o_alias, b, seq, past, row0):
    assert row0 % seq == 0
    blk0 = row0 // seq
    new_q = lambda bi, h: (blk0 + bi, h)
    return pl.pallas_call(
        _attn_sample_kernel,
        out_shape=jax.ShapeDtypeStruct(o_alias.shape, F32),
        grid=(b, MLA_HEADS),
        in_specs=[pl.BlockSpec((seq, QK_PAD), new_q),
                  pl.BlockSpec((past, QK_PAD), lambda bi, h: (bi, h)),
                  pl.BlockSpec((past, VDIM), lambda bi, h: (bi, h)),
                  pl.BlockSpec((seq, QK_PAD), new_q),
                  pl.BlockSpec((seq, VDIM), new_q),
                  pl.BlockSpec(memory_space=pl.ANY)],
        out_specs=pl.BlockSpec((seq, VDIM), new_q),
        input_output_aliases={5: 0},
        compiler_params=_params("arbitrary", "arbitrary"),
    )(q_all, k_past, v_past, k_all, v_all, o_alias)


def _out_kernel(ohg_ref, omla_ref, x_ref, g1_ref, sh2_ref, sc2_ref, monw_ref, wo_ref, n2w_ref,
                x1_ref, h2_ref, *, groups):
    hw = ohg_ref.shape[1]
    om = _rms(omla_ref[...], monw_ref[...]).astype(BF16)
    y = _dot(ohg_ref[...], wo_ref[0:hw, :]) + _dot(om, wo_ref[hw:, :])
    n2w = n2w_ref[...]
    for g in range(groups):
        rows = slice(g * GROUP, (g + 1) * GROUP)
        x1 = x_ref[rows, :] + g1_ref[g:g + 1, :] * y[rows, :]
        x1_ref[rows, :] = x1
        h2 = _rms(x1, n2w) * (1.0 + sc2_ref[g:g + 1, :]) + sh2_ref[g:g + 1, :]
        h2_ref[rows, :] = h2.astype(h2_ref.dtype)


def _out_proj(o_hg, o_mla, x, mod_g, monw, w_o, n2w, h2_dtype):
    n, d = x.shape
    tm = _pick_tile(n, 256, GROUP)
    groups = tm // GROUP
    row = lambda i: (i, 0)
    fixed = lambda i: (0, 0)
    modspec = lambda col: pl.BlockSpec((groups, d), lambda i: (i, col))
    return pl.pallas_call(
        functools.partial(_out_kernel, groups=groups),
        out_shape=[jax.ShapeDtypeStruct((n, d), F32), jax.ShapeDtypeStruct((n, d), h2_dtype)],
        grid=(n // tm,),
        in_specs=[pl.BlockSpec((tm, o_hg.shape[1]), row), pl.BlockSpec((tm, o_mla.shape[1]), row),
                  pl.BlockSpec((tm, d), row), modspec(2), modspec(3), modspec(4),
                  pl.BlockSpec(monw.shape, fixed), pl.BlockSpec(w_o.shape, fixed),
                  pl.BlockSpec(n2w.shape, fixed)],
        out_specs=[pl.BlockSpec((tm, d), row), pl.BlockSpec((tm, d), row)],
        compiler_params=_params("arbitrary"),
    )(o_hg, o_mla, x, mod_g, mod_g, mod_g, monw, w_o, n2w)


def _ffn_kernel(h_ref, wg_ref, wu_ref, wd_ref, x1_ref, g2_ref, o_ref, acc_scr, *, groups):
    j = pl.program_id(1)

    @pl.when(j == 0)
    def _():
        acc_scr[...] = jnp.zeros_like(acc_scr)

    h = h_ref[...]
    a = (_silu(_dot(h, wg_ref[...].astype(BF16))) * _dot(h, wu_ref[...].astype(BF16))).astype(BF16)
    acc_scr[...] += _dot(a, wd_ref[...].astype(BF16))

    @pl.when(j == pl.num_programs(1) - 1)
    def _():
        for g in range(groups):
            rows = slice(g * GROUP, (g + 1) * GROUP)
            o_ref[rows, :] = x1_ref[rows, :] + g2_ref[g:g + 1, :] * acc_scr[rows, :]


def _ffn(h2, wg, wu, wd, x1, mod_g):
    n, d = x1.shape
    f = wg.shape[1]
    tm = _pick_tile(n, 1280, GROUP)
    tf = _pick_tile(f, 512, LANES)
    groups = tm // GROUP
    return pl.pallas_call(
        functools.partial(_ffn_kernel, groups=groups),
        out_shape=jax.ShapeDtypeStruct((n, d), F32),
        grid=(n // tm, f // tf),
        in_specs=[pl.BlockSpec((tm, d), lambda i, j: (i, 0)),
                  pl.BlockSpec((d, tf), lambda i, j: (0, j)),
                  pl.BlockSpec((d, tf), lambda i, j: (0, j)),
                  pl.BlockSpec((tf, d), lambda i, j: (j, 0)),
                  pl.BlockSpec((tm, d), lambda i, j: (i, 0)),
                  pl.BlockSpec((groups, d), lambda i, j: (i, 5))],
        out_specs=pl.BlockSpec((tm, d), lambda i, j: (i, 0)),
        scratch_shapes=[pltpu.VMEM((tm, d), F32)],
        compiler_params=_params("arbitrary", "arbitrary"),
    )(h2, wg, wu, wd, x1, mod_g)


def _router_kernel(h_ref, rhi_ref, rlo_ref, tril_ref, pos_ref, gate_ref, cnt_ref, cnt_scr):
    i = pl.program_id(0)

    @pl.when(i == 0)
    def _():
        cnt_scr[...] = jnp.zeros_like(cnt_scr)

    h_hi, h_lo = _split_bf16(h_ref[...])
    logits = _dot(h_hi, rhi_ref[...]) + (_dot(h_hi, rlo_ref[...]) + _dot(h_lo, rhi_ref[...]))
    lane = lax.broadcasted_iota(jnp.int32, logits.shape, 1).astype(F32)
    logits = jnp.where(lane < N_EXPERTS, logits, -1e30)
    m1 = jnp.max(logits, axis=-1, keepdims=True)
    i1 = jnp.min(jnp.where(logits == m1, lane, float(LANES)), axis=-1, keepdims=True)
    hot1 = lane == i1
    rest = jnp.where(hot1, -1e30, logits)
    m2 = jnp.max(rest, axis=-1, keepdims=True)
    i2 = jnp.min(jnp.where(rest == m2, lane, float(LANES)), axis=-1, keepdims=True)
    hot2 = lane == i2
    e = jnp.exp(m2 - m1)
    gate1 = 1.0 / (1.0 + e)
    gate2 = e / (1.0 + e)
    gate_ref[...] = jnp.where(lane == 0, gate1, jnp.where(lane == 1, gate2, 0.0))
    hot = jnp.where(hot1, 1.0, jnp.where(hot2, 1.0, 0.0))
    before = _dot(tril_ref[...], hot.astype(BF16)) + cnt_scr[0:1, :]
    rank1 = jnp.sum(jnp.where(hot1, before, 0.0), axis=-1, keepdims=True)
    rank2 = jnp.sum(jnp.where(hot2, before, 0.0), axis=-1, keepdims=True)
    info = jnp.where(lane == 0, rank1, jnp.where(lane == 1, rank2,
                     jnp.where(lane == 2, i1, jnp.where(lane == 3, i2, 0.0))))
    pos_ref[...] = info.astype(jnp.int32)
    total = cnt_scr[0:1, :] + jnp.sum(hot, axis=0, keepdims=True)
    cnt_scr[...] = jnp.broadcast_to(total, cnt_scr.shape)
    cnt_ref[...] = jnp.broadcast_to(total, cnt_ref.shape).astype(jnp.int32)


def _router(h2, router):
    n, d = h2.shape
    tm = _pick_tile(n, 1280, GROUP)
    r_pad = jnp.zeros((d, LANES), F32).at[:, :N_EXPERTS].set(router)
    r_hi = r_pad.astype(BF16)
    r_lo = (r_pad - r_hi.astype(F32)).astype(BF16)
    tril = jnp.asarray(np.tril(np.ones((tm, tm), np.float32), -1), BF16)
    row = lambda i: (i, 0)
    fixed = lambda i: (0, 0)
    return pl.pallas_call(
        _router_kernel,
        out_shape=[jax.ShapeDtypeStruct((n, LANES), jnp.int32), jax.ShapeDtypeStruct((n, LANES), F32),
                   jax.ShapeDtypeStruct((8, LANES), jnp.int32)],
        grid=(n // tm,),
        in_specs=[pl.BlockSpec((tm, d), row), pl.BlockSpec((d, LANES), fixed),
                  pl.BlockSpec((d, LANES), fixed), pl.BlockSpec((tm, tm), fixed)],
        out_specs=[pl.BlockSpec((tm, LANES), row), pl.BlockSpec((tm, LANES), row),
                   pl.BlockSpec((8, LANES), fixed)],
        scratch_shapes=[pltpu.VMEM((8, LANES), F32)],
        compiler_params=_params("arbitrary"),
    )(h2, r_hi, r_lo, tril)


def _row_copy(src, src_row, dst, dst_row, sem):
    return pltpu.make_async_copy(src.at[pl.ds(src_row, 1), :], dst.at[pl.ds(dst_row, 1), :], sem)


def _scatter_kernel(cnt_ref, off_ref, nu_ref, pos_ref, h_ref, xs_ref, zero_scr, sem, *, tm, tile, n_tiles):
    i = pl.program_id(0)

    def start(r, _):
        _row_copy(h_ref, r, xs_ref, pos_ref[0, 0, 2 * r], sem).start()
        _row_copy(h_ref, r, xs_ref, pos_ref[0, 0, 2 * r + 1], sem).start()
        return 0

    lax.fori_loop(0, tm, start, 0, unroll=16)
    for _ in range(2 * tm):
        _row_copy(h_ref, 0, xs_ref, 0, sem).wait()

    @pl.when(i == pl.num_programs(0) - 1)
    def _():
        zero_scr[...] = jnp.zeros_like(zero_scr)
        for e in range(N_EXPERTS):
            cnt = cnt_ref[e]
            end = ((cnt + tile - 1) // tile) * tile

            def fill(r, _):
                _row_copy(zero_scr, 0, xs_ref, off_ref[e] + r, sem).start()
                return 0

            def fill_wait(r, _):
                _row_copy(zero_scr, 0, xs_ref, 0, sem).wait()
                return 0

            lax.fori_loop(cnt, end, fill, 0)
            lax.fori_loop(cnt, end, fill_wait, 0)

        def fill_tile(t, _):
            cp = pltpu.make_async_copy(zero_scr, xs_ref.at[pl.ds(pl.multiple_of(t * tile, tile), tile), :], sem)
            cp.start()
            cp.wait()
            return 0

        lax.fori_loop(nu_ref[0], n_tiles, fill_tile, 0)


def _scatter(h2, pos2, counts, offsets, n_used, tile, n_tiles):
    n, d = h2.shape
    tm = pos2.shape[2] // 2
    grid_spec = pltpu.PrefetchScalarGridSpec(
        num_scalar_prefetch=3,
        grid=(n // tm,),
        in_specs=[pl.BlockSpec((1, 1, 2 * tm), lambda i, *_: (i, 0, 0), memory_space=pltpu.SMEM),
                  pl.BlockSpec((tm, d), lambda i, *_: (i, 0))],
        out_specs=pl.BlockSpec(memory_space=pl.ANY),
        scratch_shapes=[pltpu.VMEM((tile, d), F32), pltpu.SemaphoreType.DMA(())],
    )
    return pl.pallas_call(
        functools.partial(_scatter_kernel, tm=tm, tile=tile, n_tiles=n_tiles),
        out_shape=jax.ShapeDtypeStruct((n_tiles * tile, d), F32),
        grid_spec=grid_spec,
        compiler_params=_params("arbitrary"),
    )(counts, offsets, n_used, pos2, h2)


def _moe_kernel(te_ref, nu_ref, x_ref, wg_ref, wu_ref, wd_ref, y_ref, xb_scr, acc_scr):
    i = pl.program_id(0)
    j = pl.program_id(1)

    @pl.when((i >= nu_ref[0]) & (j == 0))
    def _():
        y_ref[...] = jnp.zeros_like(y_ref)

    @pl.when(i < nu_ref[0])
    def _():
        @pl.when(j == 0)
        def _():
            xb_scr[...] = x_ref[...].astype(BF16)
            acc_scr[...] = jnp.zeros_like(acc_scr)

        x = xb_scr[...]
        a = (_silu(_dot(x, wg_ref[0].astype(BF16))) * _dot(x, wu_ref[0].astype(BF16))).astype(BF16)
        acc_scr[...] += _dot(a, wd_ref[0].astype(BF16))

        @pl.when(j == pl.num_programs(1) - 1)
        def _():
            y_ref[...] = acc_scr[...]


def _moe(xs, wg, wu, wd, tile_e, n_used, tm):
    rows, d = xs.shape
    f = wg.shape[2]
    tf = _pick_tile(f, 896, LANES)
    nf = f // tf

    def fcol(i, j, nu):
        return jnp.where(i < nu[0], j, nf - 1)

    grid_spec = pltpu.PrefetchScalarGridSpec(
        num_scalar_prefetch=2,
        grid=(rows // tm, nf),
        in_specs=[pl.BlockSpec((tm, d), lambda i, j, te, nu: (i, 0)),
                  pl.BlockSpec((1, d, tf), lambda i, j, te, nu: (te[i], 0, fcol(i, j, nu))),
                  pl.BlockSpec((1, d, tf), lambda i, j, te, nu: (te[i], 0, fcol(i, j, nu))),
                  pl.BlockSpec((1, tf, d), lambda i, j, te, nu: (te[i], fcol(i, j, nu), 0))],
        out_specs=pl.BlockSpec((tm, d), lambda i, j, te, nu: (i, 0)),
        scratch_shapes=[pltpu.VMEM((tm, d), BF16), pltpu.VMEM((tm, d), F32)],
    )
    return pl.pallas_call(
        _moe_kernel,
        out_shape=jax.ShapeDtypeStruct((rows, d), F32),
        grid_spec=grid_spec,
        compiler_params=_params("arbitrary", "arbitrary"),
    )(tile_e, n_used, xs, wg, wu, wd)


def _combine_kernel(pos_ref, gate_ref, x1_ref, g2_ref, y_ref, *rest, tm, groups, head_tiles):
    *o_refs, buf_scr, sem = rest

    def start(r, _):
        pltpu.make_async_copy(y_ref.at[pl.ds(pos_ref[0, 0, 2 * r], 1), :],
                              buf_scr.at[0, pl.ds(r, 1), :], sem).start()
        pltpu.make_async_copy(y_ref.at[pl.ds(pos_ref[0, 0, 2 * r + 1], 1), :],
                              buf_scr.at[1, pl.ds(r, 1), :], sem).start()
        return 0

    lax.fori_loop(0, tm, start, 0, unroll=16)
    for _ in range(2 * tm):
        pltpu.make_async_copy(y_ref.at[pl.ds(0, 1), :], buf_scr.at[0, pl.ds(0, 1), :], sem).wait()
    def emit(o_ref):
        for g in range(groups):
            rows = slice(g * GROUP, (g + 1) * GROUP)
            f = gate_ref[rows, 0:1] * buf_scr[0, rows, :] + gate_ref[rows, 1:2] * buf_scr[1, rows, :]
            o_ref[rows, :] = x1_ref[rows, :] + g2_ref[g:g + 1, :] * f

    if len(o_refs) == 1:
        emit(o_refs[0])
    else:
        i = pl.program_id(0)
        pl.when(i < head_tiles)(lambda: emit(o_refs[0]))
        pl.when(i >= head_tiles)(lambda: emit(o_refs[1]))


def _combine(y, pos2, gates, x1, mod_g, split):
    n, d = x1.shape
    tm = pos2.shape[2] // 2
    groups = tm // GROUP
    if split is None or split % tm or (n - split) % tm:
        head_tiles = 0
        out_shape = jax.ShapeDtypeStruct((n, d), F32)
        out_specs = pl.BlockSpec((tm, d), lambda i: (i, 0))
    else:
        head_tiles = split // tm
        out_shape = [jax.ShapeDtypeStruct((split, d), F32), jax.ShapeDtypeStruct((n - split, d), F32)]
        out_specs = [pl.BlockSpec((tm, d), lambda i: (jnp.minimum(i, head_tiles - 1), 0)),
                     pl.BlockSpec((tm, d), lambda i: (jnp.maximum(i - head_tiles, 0), 0))]
    out = pl.pallas_call(
        functools.partial(_combine_kernel, tm=tm, groups=groups, head_tiles=head_tiles),
        out_shape=out_shape,
        grid=(n // tm,),
        in_specs=[pl.BlockSpec((1, 1, 2 * tm), lambda i: (i, 0, 0), memory_space=pltpu.SMEM),
                  pl.BlockSpec((tm, LANES), lambda i: (i, 0)),
                  pl.BlockSpec((tm, d), lambda i: (i, 0)),
                  pl.BlockSpec((groups, d), lambda i: (i, 5)),
                  pl.BlockSpec(memory_space=pl.ANY)],
        out_specs=out_specs,
        scratch_shapes=[pltpu.VMEM((2, tm, d), F32), pltpu.SemaphoreType.DMA(())],
        compiler_params=_params("arbitrary"),
    )(pos2, gates, x1, mod_g, y)
    if head_tiles:
        return out
    return out if split is None else [out[:split], out[split:]]


def _routed_ffn(h2, router, wg, wu, wd, x1, mod_g, split=None):
    n, d = x1.shape
    tile = _pick_tile(n, 640, GROUP)
    n_tiles = (2 * n) // tile + N_EXPERTS
    info, gates, counts = _router(h2, router)
    counts = counts[0, :N_EXPERTS]
    tiles_per = (counts + tile - 1) // tile
    ends = jnp.cumsum(tiles_per)
    offsets = ((ends - tiles_per) * tile).astype(jnp.int32)
    n_used = ends[-1:].astype(jnp.int32)
    t = jnp.minimum(jnp.arange(n_tiles, dtype=jnp.int32), n_used - 1)
    tile_e = jnp.sum(t[:, None] >= ends[None, :], axis=1).astype(jnp.int32)
    tm = _pick_tile(n, 256, GROUP)
    pos2 = (offsets[info[:, 2:4]] + info[:, 0:2]).reshape(n // tm, 1, 2 * tm)
    xs = _scatter(h2, pos2, counts, offsets, n_used, tile, n_tiles)
    y = _moe(xs, wg, wu, wd, tile_e, n_used, tile)
    return _combine(y, pos2, gates, x1, mod_g, split)


def _swap_halves(a):
    half = a.shape[-1] // 2
    return jnp.concatenate([a[..., half:], a[..., :half]], axis=-1)


def kernel(x_prompt, x_sample, c_prompt, c_sample, state_hgrn, cache_ckv, cache_kpe, ada_w, ada_b, norm1_w, norm2_w, w_in, hg_lower_bounds, hg_onorm_w, mla_qa_norm_w, mla_wqb, mla_kva_norm_w, mla_wkvb, mla_qn_nope_w, mla_qn_rope_w, mla_kn_nope_w, mla_kn_rope_w, mla_onorm_w, w_o, ffn_w_gate, ffn_w_up, ffn_w_down, moe_router, moe_w_gate, moe_w_up, moe_w_down):
    bp, sp, d = x_prompt.shape
    bs, ss, _ = x_sample.shape
    depth = ada_w.shape[0]
    past = cache_ckv.shape[2]
    n_p, n_s = bp * sp, bs * ss
    n = n_p + n_s
    hw = HG_HEADS * HG_DIM
    q_lora = mla_qa_norm_w.shape[1]
    kv_lora = mla_kva_norm_w.shape[1]
    assert sp % GROUP == 0 and ss % GROUP == 0 and n_p % ss == 0
    scale = float((NOPE + ROPE) ** -0.5)

    x = jnp.concatenate([x_prompt.reshape(n_p, d), x_sample.reshape(n_s, d)], axis=0)
    mod = _modulation(jnp.concatenate([c_prompt, c_sample], axis=0), ada_w, ada_b)
    mod_g = jnp.concatenate([jnp.repeat(mod[:, :bp], sp // GROUP, axis=1),
                             jnp.repeat(mod[:, bp:], ss // GROUP, axis=1)], axis=1)

    lw = jax.nn.softmax(hg_lower_bounds.astype(F32), axis=0)
    lower = jnp.clip(jnp.cumsum(lw, axis=0) - lw[0], 0.0, 1.0)

    half = ROPE // 2
    inv_freq = ROPE_BASE ** (-jnp.arange(half, dtype=F32) / half)

    def rope_rows(pos):
        ang = pos.astype(F32)[:, None] * inv_freq[None, :]
        return jnp.concatenate([jnp.cos(ang), jnp.cos(ang), -jnp.sin(ang), jnp.sin(ang)], axis=-1)

    rope = jnp.concatenate([jnp.tile(rope_rows(jnp.arange(sp)), (bp, 1)),
                            jnp.tile(rope_rows(past + jnp.arange(ss)), (bs, 1))], axis=0)

    row2 = lambda a: a.reshape(1, -1)
    pair = lambda w: row2(jnp.concatenate([w, _swap_halves(w)]))
    zeros_state = jnp.zeros((bp,) + state_hgrn.shape[2:], F32)

    new_s_p, new_s_s, new_ckv, new_kpe = [], [], [], []
    for l in range(depth):
        kpe_cols = w_in[l][:, -ROPE:]
        w_in_l = jnp.concatenate([w_in[l], _swap_halves(kpe_cols)], axis=1).astype(BF16)
        wqb3 = mla_wqb[l].reshape(q_lora, MLA_HEADS, NOPE + ROPE)
        q_rope = wqb3[:, :, NOPE:]
        wqb_l = jnp.concatenate(
            [wqb3[:, :, :NOPE].reshape(q_lora, MLA_HEADS * NOPE),
             jnp.concatenate([q_rope, _swap_halves(q_rope)], axis=-1).reshape(q_lora, MLA_HEADS * LANES)],
            axis=1).astype(BF16)
        wkvb_l = mla_wkvb[l].astype(BF16)
        knw = row2(mla_kn_nope_w[l])

        hq, hk, lf, hv, hg, ckv, kpe, q_all, k_all, v_all = _front(
            x, mod_g[l], row2(norm1_w[l]), w_in_l, row2(lower[l]), row2(mla_qa_norm_w[l]), wqb_l,
            row2(mla_qn_nope_w[l]), pair(mla_qn_rope_w[l]), row2(mla_kva_norm_w[l]), wkvb_l, knw,
            pair(mla_kn_rope_w[l]), rope, scale)

        onw = row2(hg_onorm_w[l])
        o_hg = jnp.zeros((n, hw), BF16)
        o_hg, s_p = _hgrn(hq, hk, lf, hv, hg, zeros_state, onw, o_hg, 0, sp)
        o_hg, s_s = _hgrn(hq, hk, lf, hv, hg, state_hgrn[l].astype(F32), onw, o_hg, n_p, ss)

        kpe_past = jnp.pad(cache_kpe[l].reshape(bs * past, ROPE).astype(BF16), ((0, 0), (0, LANES - ROPE)))
        k_past, v_past = _kv_up(cache_ckv[l].reshape(bs * past, kv_lora), kpe_past, wkvb_l, knw)
        o_mla = _attn_prompt(q_all, k_all, v_all, jnp.zeros((n, MLA_HEADS * VDIM), F32), bp, sp)
        o_mla = _attn_sample(q_all, k_all, v_all, k_past, v_past, o_mla, bs, ss, past, n_p)

        moe_layer = l % 2 == 1
        x1, h2 = _out_proj(o_hg, o_mla, x, mod_g[l], row2(mla_onorm_w[l]), w_o[l].astype(BF16),
                           row2(norm2_w[l]), F32 if moe_layer else BF16)
        j = l // 2
        if moe_layer:
            x = _routed_ffn(h2, moe_router[j], moe_w_gate[j], moe_w_up[j], moe_w_down[j], x1, mod_g[l],
                            split=n_p if l == depth - 1 else None)
        else:
            x = _ffn(h2, ffn_w_gate[j], ffn_w_up[j], ffn_w_down[j], x1, mod_g[l])

        new_s_p.append(s_p)
        new_s_s.append(s_s)
        new_ckv.append(ckv)
        new_kpe.append(kpe)

    ckv_all = jnp.stack(new_ckv)
    kpe_all = jnp.stack(new_kpe)
    y_p, y_s = x if isinstance(x, (list, tuple)) else (x[:n_p], x[n_p:])
    return (y_p.reshape(bp, sp, d), y_s.reshape(bs, ss, d),
            jnp.stack(new_s_p), ckv_all[:, :n_p].reshape(depth, bp, sp, kv_lora),
            kpe_all[:, :n_p].reshape(depth, bp, sp, ROPE),
            jnp.stack(new_s_s), ckv_all[:, n_p:].reshape(depth, bs, ss, kv_lora),
            kpe_all[:, n_p:].reshape(depth, bs, ss, ROPE))
```
